```python
import jax, jax.numpy as jnp
from jax import lax
import numpy as np

D_MODEL = 2048
BATCH = 1
SEQ = 16384
DEPTH = 1

HEAD_DIM = 128
ROT_DIM = HEAD_DIM // 4
ROPE_THETA = 500000.0
DIL_PATTERNS = ((128, 1), (512, 4), (2048, 16))
N_DIL_GROUPS = len(DIL_PATTERNS)
HEADS_PER_GROUP = 4
ATTN_WIDTH = N_DIL_GROUPS * HEADS_PER_GROUP * HEAD_DIM
ATTN_OUT = HEADS_PER_GROUP * HEAD_DIM
ATTN_BLOCK = 128
NEG_INF = -1e30
CHUNK = 128
SGU_GROUPS = 8
SGU_GROUP_DIM = 128
SGU_WIDTH = SGU_GROUPS * SGU_GROUP_DIM
IN_WIDTH = 3 * ATTN_WIDTH + 2 * SGU_WIDTH + 2 * D_MODEL
IN_SPLITS = (ATTN_WIDTH, 2 * ATTN_WIDTH, 3 * ATTN_WIDTH, 3 * ATTN_WIDTH + 2 * SGU_WIDTH,
             3 * ATTN_WIDTH + 2 * SGU_WIDTH + D_MODEL)
N_EXPERTS = 32
TOP_K = 4
D_EXPERT = D_MODEL
SWIGLU_LIMIT = 7.0
SWIGLU_ALPHA = 1.702
MOE_BLOCK = 128
LN_EPS = 1e-5
DEEPNORM_ALPHA = (2 * DEPTH) ** 0.25
DEEPNORM_BETA = (8 * DEPTH) ** -0.25

kernel_name = 'hybrid_dilated_sgu_moe_deepnorm_encoder'


def layer_norm(x, g, b):
    xf = x.astype(jnp.float32)
    mu = jnp.mean(xf, -1, keepdims=True)
    var = jnp.mean(jnp.square(xf - mu), -1, keepdims=True)
    return ((xf - mu) * lax.rsqrt(var + LN_EPS) * g + b).astype(x.dtype)


def rope_partial(t, cos, sin):
    half = ROT_DIM // 2
    r1, r2, rest = t[..., :half], t[..., half:ROT_DIM], t[..., ROT_DIM:]
    rot = jnp.concatenate([r1 * cos - r2 * sin, r2 * cos + r1 * sin], -1).astype(t.dtype)
    return jnp.concatenate([rot, rest], -1)


def banded_attention(q, k, v, half):
    N, L, H, hd = q.shape
    nb = -(-L // ATTN_BLOCK)
    Lp = nb * ATTN_BLOCK
    band = ATTN_BLOCK + 2 * half
    qb = jnp.pad(q, ((0, 0), (0, Lp - L), (0, 0), (0, 0))).reshape(N, nb, ATTN_BLOCK, H, hd)
    pad_kv = ((0, 0), (half, Lp - L + half), (0, 0), (0, 0))
    kp, vp = jnp.pad(k, pad_kv), jnp.pad(v, pad_kv)
    idx = (jnp.arange(nb) * ATTN_BLOCK)[:, None] + jnp.arange(band)[None, :]
    kb, vb = kp[:, idx], vp[:, idx]
    s = jnp.einsum('nbqhd,nbkhd->nbhqk', qb, kb).astype(jnp.float32) * (hd ** -0.5)
    qpos = (jnp.arange(nb) * ATTN_BLOCK)[:, None] + jnp.arange(ATTN_BLOCK)[None, :]
    kpos = idx - half
    rel = kpos[:, None, :] - qpos[:, :, None]
    mask = (jnp.abs(rel) <= half) & (kpos[:, None, :] >= 0) & (kpos[:, None, :] < L)
    s = jnp.where(mask[None, :, None], s, NEG_INF)
    m = jnp.max(s, -1, keepdims=True)
    p = jnp.exp(s - m)
    den = jnp.sum(p, -1, keepdims=True)
    lse = (m + jnp.log(den))[..., 0]
    o = jnp.einsum('nbhqk,nbkhd->nbqhd', (p / den).astype(v.dtype), vb)
    o = o.reshape(N, Lp, H, hd)[:, :L]
    lse = lse.transpose(0, 1, 3, 2).reshape(N, Lp, H)[:, :L]
    return o, lse


def dilated_window_attention(q, k, v, window, dilation):
    B, S, H, hd = q.shape
    L = S // dilation
    def split(t):
        return t.reshape(B, L, dilation, H, hd).transpose(0, 2, 1, 3, 4).reshape(B * dilation, L, H, hd)
    o, lse = banded_attention(split(q), split(k), split(v), window // (2 * dilation))
    o = o.reshape(B, dilation, L, H, hd).transpose(0, 2, 1, 3, 4).reshape(B, S, H, hd)
    lse = lse.reshape(B, dilation, L, H).transpose(0, 2, 1, 3).reshape(B, S, H)
    return o, lse


def token_mixers(x, positions, w_in, w_attn_out, sgu_ln_g, sgu_ln_b, sgu_w, sgu_b, w_sgu_out, w_out):
    B, S, _ = x.shape
    proj = x @ w_in
    q, k, v, z, g_attn, g_sgu = jnp.split(proj, IN_SPLITS, axis=-1)
    inv_freq = ROPE_THETA ** (-jnp.arange(0, ROT_DIM, 2, dtype=jnp.float32) / ROT_DIM)
    ang = positions.astype(jnp.float32)[..., None] * inv_freq
    cos = jnp.cos(ang)[:, :, None, None, :]
    sin = jnp.sin(ang)[:, :, None, None, :]
    shp = (B, S, N_DIL_GROUPS, HEADS_PER_GROUP, HEAD_DIM)
    q = rope_partial(q.reshape(shp), cos, sin)
    k = rope_partial(k.reshape(shp), cos, sin)
    v = v.reshape(shp)
    outs, lses = [], []
    for gi, (window, dilation) in enumerate(DIL_PATTERNS):
        o_g, lse_g = dilated_window_attention(q[:, :, gi], k[:, :, gi], v[:, :, gi], window, dilation)
        outs.append(o_g)
        lses.append(lse_g)
    wts = jax.nn.softmax(jnp.stack(lses, 0), axis=0)
    o_a = jnp.sum(wts[..., None].astype(v.dtype) * jnp.stack(outs, 0), axis=0)
    y_a = o_a.reshape(B, S, ATTN_OUT) @ w_attn_out
    z = jax.nn.gelu(z, approximate=False)
    u, vv = z[..., :SGU_WIDTH], z[..., SGU_WIDTH:]
    vv = layer_norm(vv, sgu_ln_g, sgu_ln_b)
    vc = vv.reshape(B, S // CHUNK, CHUNK, SGU_GROUPS, SGU_GROUP_DIM)
    mixed = jnp.einsum('gts,bnsgc->bntgc', sgu_w, vc) + sgu_b.T[:, :, None]
    y_b = (u * mixed.reshape(B, S, SGU_WIDTH)) @ w_sgu_out
    merged = jax.nn.sigmoid(g_attn) * y_a + jax.nn.sigmoid(g_sgu) * y_b
    return merged @ w_out


def moe(x2d, w_router, b_router, w_gate_up, b_gate_up, w_down, b_down):
    T, D = x2d.shape
    logits = (x2d @ w_router + b_router).astype(jnp.float32)
    top_val, top_idx = lax.top_k(logits, TOP_K)
    gate = jax.nn.softmax(top_val, axis=-1)
    n = T * TOP_K
    flat_e = top_idx.reshape(-1)
    flat_t = jnp.repeat(jnp.arange(T, dtype=jnp.int32), TOP_K)
    flat_w = gate.reshape(-1)
    order = jnp.argsort(flat_e)
    se, st, sw = flat_e[order], flat_t[order], flat_w[order]
    counts = jnp.bincount(flat_e, length=N_EXPERTS)
    padded = (counts + MOE_BLOCK - 1) // MOE_BLOCK * MOE_BLOCK
    starts = jnp.cumsum(counts) - counts
    pends = jnp.cumsum(padded)
    pstarts = pends - padded
    dest = pstarts[se] + (jnp.arange(n) - starts[se])
    n_blocks = -(-n // MOE_BLOCK) + N_EXPERTS
    rows = n_blocks * MOE_BLOCK
    row_tok = jnp.full((rows,), T, jnp.int32).at[dest].set(st)
    row_w = jnp.zeros((rows,), jnp.float32).at[dest].set(sw)
    block_e = jnp.clip(jnp.searchsorted(pends, jnp.arange(n_blocks) * MOE_BLOCK, side='right'),
                       0, N_EXPERTS - 1)
    x_pad = jnp.concatenate([x2d, jnp.zeros((1, D), x2d.dtype)], 0)
    xb = x_pad[row_tok].reshape(n_blocks, MOE_BLOCK, D)

    def expert_block(args):
        xblk, e = args
        gu = xblk @ w_gate_up[e] + b_gate_up[e]
        g, up = gu[:, :D_EXPERT], gu[:, D_EXPERT:]
        g = jnp.minimum(g, SWIGLU_LIMIT)
        up = jnp.clip(up, -SWIGLU_LIMIT, SWIGLU_LIMIT)
        h = (up + 1.0) * (g * jax.nn.sigmoid(SWIGLU_ALPHA * g))
        return h @ w_down[e] + b_down[e]

    yb = lax.map(expert_block, (xb, block_e)).reshape(rows, D)
    y = jax.ops.segment_sum(yb.astype(jnp.float32) * row_w[:, None], row_tok, num_segments=T + 1)[:T]
    return y.astype(x2d.dtype)


def setup_inputs(seed: int = 0) -> dict:
    key = jax.random.key(seed)
    ks = jax.random.split(key, 24)
    f32 = jnp.float32
    def nrm(k, shape, scale):
        return jax.random.normal(k, shape, f32) * scale
    L = DEPTH
    return {
        'x': nrm(ks[0], (BATCH, SEQ, D_MODEL), 1.0),
        'positions': jnp.broadcast_to(jnp.arange(SEQ, dtype=jnp.int32)[None, :], (BATCH, SEQ)),
        'w_in': nrm(ks[1], (L, D_MODEL, IN_WIDTH), D_MODEL ** -0.5),
        'w_attn_out': nrm(ks[2], (L, ATTN_OUT, D_MODEL), ATTN_OUT ** -0.5),
        'sgu_ln_g': 1.0 + nrm(ks[3], (L, SGU_WIDTH), 0.02),
        'sgu_ln_b': nrm(ks[4], (L, SGU_WIDTH), 0.02),
        'sgu_w': nrm(ks[5], (L, SGU_GROUPS, CHUNK, CHUNK), CHUNK ** -0.5),
        'sgu_b': 1.0 + nrm(ks[6], (L, SGU_GROUPS, CHUNK), 0.02),
        'w_sgu_out': nrm(ks[7], (L, SGU_WIDTH, D_MODEL), SGU_WIDTH ** -0.5),
        'w_out': nrm(ks[8], (L, D_MODEL, D_MODEL), DEEPNORM_BETA * D_MODEL ** -0.5),
        'ln1_g': 1.0 + nrm(ks[9], (L, D_MODEL), 0.02),
        'ln1_b': nrm(ks[10], (L, D_MODEL), 0.02),
        'w_router': nrm(ks[11], (L, D_MODEL, N_EXPERTS), D_MODEL ** -0.5),
        'b_router': nrm(ks[12], (L, N_EXPERTS), 0.01),
        'w_gate_up': nrm(ks[13], (L, N_EXPERTS, D_MODEL, 2 * D_EXPERT), D_MODEL ** -0.5),
        'b_gate_up': nrm(ks[14], (L, N_EXPERTS, 2 * D_EXPERT), 0.02),
        'w_down': nrm(ks[15], (L, N_EXPERTS, D_EXPERT, D_MODEL), DEEPNORM_BETA * D_EXPERT ** -0.5),
        'b_down': nrm(ks[16], (L, N_EXPERTS, D_MODEL), 0.02),
        'ln2_g': 1.0 + nrm(ks[17], (L, D_MODEL), 0.02),
        'ln2_b': nrm(ks[18], (L, D_MODEL), 0.02),
    }


def reference(x, positions, w_in, w_attn_out, sgu_ln_g, sgu_ln_b, sgu_w, sgu_b, w_sgu_out, w_out,
              ln1_g, ln1_b, w_router, b_router, w_gate_up, b_gate_up, w_down, b_down, ln2_g, ln2_b):
    B, S, D = x.shape
    for l in range(DEPTH):
        h = token_mixers(x, positions, w_in[l], w_attn_out[l], sgu_ln_g[l], sgu_ln_b[l], sgu_w[l],
                         sgu_b[l], w_sgu_out[l], w_out[l])
        x = layer_norm(DEEPNORM_ALPHA * x + h, ln1_g[l], ln1_b[l])
        y = moe(x.reshape(B * S, D), w_router[l], b_router[l], w_gate_up[l], b_gate_up[l],
                w_down[l], b_down[l]).reshape(B, S, D)
        x = layer_norm(DEEPNORM_ALPHA * x + y, ln2_g[l], ln2_b[l])
    return x
```

```python
import functools

import numpy as np
import jax
import jax.numpy as jnp
from jax import lax
from jax.experimental import pallas as pl
from jax.experimental.pallas import tpu as pltpu

F32 = jnp.float32
BF16 = jnp.bfloat16

D_MODEL = 2048
HEAD_DIM = 128
ROT_DIM = HEAD_DIM // 4
ROPE_THETA = 500000.0
DIL_PATTERNS = ((128, 1), (512, 4), (2048, 16))
N_GROUPS = len(DIL_PATTERNS)
HEADS = 4
GROUP_W = HEADS * HEAD_DIM
ATTN_WIDTH = N_GROUPS * GROUP_W
CHUNK = 128
SGU_GROUPS = 8
SGU_WIDTH = SGU_GROUPS * CHUNK
IN_WIDTH = 3 * ATTN_WIDTH + 2 * SGU_WIDTH + 2 * D_MODEL
N_EXPERTS = 32
TOP_K = 4
D_EXPERT = D_MODEL
SWIGLU_LIMIT = 7.0
SWIGLU_ALPHA = 1.702
LN_EPS = 1e-5
DEPTH = 1
DEEPNORM_ALPHA = (2 * DEPTH) ** 0.25
NEG_INF = -1e30

LANES = 128
V7X_VMEM_BYTES = 64 * 1024 * 1024
VMEM_LIMIT = V7X_VMEM_BYTES - 8 * 1024 * 1024

PROJ_TM = 512
PROJ_TN = 512
ATTN_BLK = 128
TAIL_TM = 256
MOE_TM = 256
COMB_TM = 128

Z_BLK0, GA_BLK0, GB_BLK0, Q_BLK0, K_BLK0, V_BLK0 = 0, 4, 8, 12, 15, 18
N_PROJ_BLKS = IN_WIDTH // PROJ_TN
assert N_PROJ_BLKS == 21


def _cparams(sem):
    return pltpu.CompilerParams(dimension_semantics=sem, vmem_limit_bytes=VMEM_LIMIT)


def _layer_norm(v, g, b):
    mu = jnp.mean(v, axis=-1, keepdims=True)
    c = v - mu
    var = jnp.mean(c * c, axis=-1, keepdims=True)
    return c * lax.rsqrt(var + LN_EPS) * g + b


def _inproj_kernel(x_ref, w_ref, rc_ref, ra_ref, rb_ref, o_ref, xb_ref):
    j = pl.program_id(1)

    @pl.when(j == 0)
    def _():
        xb_ref[...] = x_ref[...].astype(BF16)

    acc = jnp.dot(xb_ref[...], w_ref[...], preferred_element_type=F32)

    @pl.when(j < GA_BLK0)
    def _():
        o_ref[...] = (0.5 * acc * (1.0 + lax.erf(acc * (2.0 ** -0.5)))).astype(BF16)

    @pl.when((j >= GA_BLK0) & (j < Q_BLK0))
    def _():
        o_ref[...] = (1.0 / (1.0 + jnp.exp(-acc))).astype(BF16)

    @pl.when((j >= Q_BLK0) & (j < V_BLK0))
    def _():
        c = jnp.concatenate([rc_ref[...]] * HEADS, axis=1)
        a = jnp.concatenate([ra_ref[...]] * HEADS, axis=1)
        b = jnp.concatenate([rb_ref[...]] * HEADS, axis=1)
        half = ROT_DIM // 2
        up = pltpu.roll(acc, PROJ_TN - half, axis=1)
        dn = pltpu.roll(acc, half, axis=1)
        o_ref[...] = (acc * c + up * a + dn * b).astype(BF16)

    @pl.when(j >= V_BLK0)
    def _():
        o_ref[...] = acc.astype(BF16)


def _inproj(x2d, w_cat, rc, ra, rb):
    S = x2d.shape[0]
    return pl.pallas_call(
        _inproj_kernel,
        grid=(S // PROJ_TM, N_PROJ_BLKS),
        in_specs=[
            pl.BlockSpec((PROJ_TM, D_MODEL), lambda i, j: (i, 0)),
            pl.BlockSpec((D_MODEL, PROJ_TN), lambda i, j: (0, j)),
            pl.BlockSpec((PROJ_TM, LANES), lambda i, j: (i, 0)),
            pl.BlockSpec((PROJ_TM, LANES), lambda i, j: (i, 0)),
            pl.BlockSpec((PROJ_TM, LANES), lambda i, j: (i, 0)),
        ],
        out_specs=pl.BlockSpec((PROJ_TM, PROJ_TN), lambda i, j: (i, j)),
        out_shape=jax.ShapeDtypeStruct((S, IN_WIDTH), BF16),
        scratch_shapes=[pltpu.VMEM((PROJ_TM, D_MODEL), BF16)],
        compiler_params=_cparams(("arbitrary", "arbitrary")),
        name="inproj",
    )(x2d, w_cat, rc, ra, rb)


def _attn_kernel(q_ref, kp_ref, kc_ref, kn_ref, vp_ref, vc_ref, vn_ref, band_ref, o_ref, l_ref):
    b = pl.program_id(1)
    nb = pl.num_programs(1)
    q = q_ref[0]
    k = jnp.concatenate([kp_ref[0], kc_ref[0], kn_ref[0]], axis=0)
    v = jnp.concatenate([vp_ref[0], vc_ref[0], vn_ref[0]], axis=0)
    col = lax.broadcasted_iota(jnp.int32, (ATTN_BLK, 3 * ATTN_BLK), 1)
    lo = jnp.where(b > 0, 0, ATTN_BLK)
    hi = jnp.where(b < nb - 1, 3 * ATTN_BLK, 2 * ATTN_BLK)
    bias = jnp.where((col >= lo) & (col < hi), band_ref[...], NEG_INF)
    lane = lax.broadcasted_iota(jnp.int32, (ATTN_BLK, LANES), 1)
    lse_tile = jnp.zeros((ATTN_BLK, LANES), F32)
    scale = HEAD_DIM ** -0.5
    for h in range(HEADS):
        hs = slice(h * HEAD_DIM, (h + 1) * HEAD_DIM)
        s = lax.dot_general(q[:, hs], k[:, hs], (((1,), (1,)), ((), ())),
                            preferred_element_type=F32) * scale + bias
        m = jnp.max(s, axis=-1, keepdims=True)
        p = jnp.exp(s - m)
        den = jnp.sum(p, axis=-1, keepdims=True)
        o = jnp.dot(p.astype(BF16), v[:, hs], preferred_element_type=F32) / den
        o_ref[0, :, hs] = o.astype(BF16)
        lse_tile = jnp.where(lane == h, m + jnp.log(den), lse_tile)
    l_ref[0] = lse_tile


def _band_bias():
    i = np.arange(ATTN_BLK)[:, None]
    c = np.arange(3 * ATTN_BLK)[None, :]
    rel = c - ATTN_BLK - i
    return jnp.asarray(np.where(np.abs(rel) <= ATTN_BLK // 2, 0.0, NEG_INF), F32)


def _attention(arr, qb, kb, vb):
    d, L, _ = arr.shape
    nb = L // ATTN_BLK
    blk = (1, ATTN_BLK, GROUP_W)

    def spec(cb, off):
        return pl.BlockSpec(blk, lambda r, b: (r, jnp.clip(b + off, 0, nb - 1), cb))

    return pl.pallas_call(
        _attn_kernel,
        grid=(d, nb),
        in_specs=[spec(qb, 0), spec(kb, -1), spec(kb, 0), spec(kb, 1),
                  spec(vb, -1), spec(vb, 0), spec(vb, 1),
                  pl.BlockSpec((ATTN_BLK, 3 * ATTN_BLK), lambda r, b: (0, 0))],
        out_specs=[pl.BlockSpec(blk, lambda r, b: (r, b, 0)),
                   pl.BlockSpec((1, ATTN_BLK, LANES), lambda r, b: (r, b, 0))],
        out_shape=[jax.ShapeDtypeStruct((d, L, GROUP_W), BF16),
                   jax.ShapeDtypeStruct((d, L, LANES), F32)],
        compiler_params=_cparams(("arbitrary", "arbitrary")),
        name=f"attn_d{d}",
    )(arr, arr, arr, arr, arr, arr, arr, _band_bias())


def _tail_kernel(z_ref, ga_ref, gb_ref, o1_ref, o2_ref, o3_ref, l1_ref, l2_ref, l3_ref, x_ref,
                 sw_ref, sbias_ref, slg_ref, slb_ref, wso_ref, wao_ref, wo_ref,
                 g1_ref, b1_ref, wr_ref, br_ref,
                 x1_ref, idx_ref, gate_ref):
    tm = z_ref.shape[0]
    z = z_ref[...]
    u = z[:, :SGU_WIDTH].astype(F32)
    vn = _layer_norm(z[:, SGU_WIDTH:].astype(F32), slg_ref[...], slb_ref[...]).astype(BF16)
    rows = []
    for c in range(tm // CHUNK):
        cols = []
        for g in range(SGU_GROUPS):
            blk = vn[c * CHUNK:(c + 1) * CHUNK, g * CHUNK:(g + 1) * CHUNK]
            cols.append(jnp.dot(sw_ref[g], blk, preferred_element_type=F32))
        rows.append(jnp.concatenate(cols, axis=1) + sbias_ref[...])
    mixed = jnp.concatenate(rows, axis=0)
    y_b = jnp.dot((u * mixed).astype(BF16), wso_ref[...], preferred_element_type=F32)

    l1, l2, l3 = l1_ref[...], l2_ref[...], l3_ref[...]
    m = jnp.maximum(jnp.maximum(l1, l2), l3)
    e1, e2, e3 = jnp.exp(l1 - m), jnp.exp(l2 - m), jnp.exp(l3 - m)
    inv = 1.0 / (e1 + e2 + e3)
    w1, w2, w3 = e1 * inv, e2 * inv, e3 * inv
    parts = []
    for h in range(HEADS):
        hs = slice(h * HEAD_DIM, (h + 1) * HEAD_DIM)
        parts.append(w1[:, h:h + 1] * o1_ref[:, hs].astype(F32)
                     + w2[:, h:h + 1] * o2_ref[:, hs].astype(F32)
                     + w3[:, h:h + 1] * o3_ref[:, hs].astype(F32))
    o_a = jnp.concatenate(parts, axis=1).astype(BF16)
    y_a = jnp.dot(o_a, wao_ref[...], preferred_element_type=F32)

    merged = ga_ref[...].astype(F32) * y_a + gb_ref[...].astype(F32) * y_b
    hmix = jnp.dot(merged.astype(BF16), wo_ref[...], preferred_element_type=F32)
    x1 = _layer_norm(DEEPNORM_ALPHA * x_ref[...] + hmix, g1_ref[...], b1_ref[...])
    x1_ref[...] = x1

    logits = jnp.dot(x1.astype(BF16), wr_ref[...], preferred_element_type=F32) + br_ref[...]
    lane = lax.broadcasted_iota(jnp.int32, (tm, LANES), 1)
    lane_f = lane.astype(F32)
    vals = jnp.where(lane < N_EXPERTS, logits, -jnp.inf)
    idx_tile = jnp.zeros((tm, LANES), F32)
    val_tile = jnp.zeros((tm, LANES), F32)
    top0 = None
    for kk in range(TOP_K):
        mk = jnp.max(vals, axis=-1, keepdims=True)
        ik = jnp.min(jnp.where(vals == mk, lane_f, float(LANES)), axis=-1, keepdims=True)
        vals = jnp.where(lane_f == ik, -jnp.inf, vals)
        if kk == 0:
            top0 = mk
        idx_tile = jnp.where(lane == kk, ik, idx_tile)
        val_tile = jnp.where(lane == kk, jnp.exp(mk - top0), val_tile)
    idx_ref[...] = idx_tile
    gate_ref[...] = val_tile / jnp.sum(val_tile, axis=-1, keepdims=True)


def _tail(proj, o1, o2, o3, l1, l2, l3, x2d, sw, sbias, slg, slb, wso, wao, wo, g1, b1, wr, br):
    S = x2d.shape[0]
    tm = TAIL_TM
    row = lambda w: pl.BlockSpec((tm, w), lambda i: (i, 0))

    def const(shape):
        nd = len(shape)
        return pl.BlockSpec(shape, lambda i: (0,) * nd, pipeline_mode=pl.Buffered(1))

    wide = lambda cb: pl.BlockSpec((tm, D_MODEL), lambda i: (i, cb))
    return pl.pallas_call(
        _tail_kernel,
        grid=(S // tm,),
        in_specs=[wide(0), wide(1), wide(2),
                  row(GROUP_W), row(GROUP_W), row(GROUP_W), row(LANES), row(LANES), row(LANES),
                  row(D_MODEL),
                  const(sw.shape), const(sbias.shape), const(slg.shape), const(slb.shape),
                  const(wso.shape), const(wao.shape), const(wo.shape),
                  const(g1.shape), const(b1.shape), const(wr.shape), const(br.shape)],
        out_specs=[row(D_MODEL), row(LANES), row(LANES)],
        out_shape=[jax.ShapeDtypeStruct((S, D_MODEL), F32),
                   jax.ShapeDtypeStruct((S, LANES), F32),
                   jax.ShapeDtypeStruct((S, LANES), F32)],
        compiler_params=_cparams(("arbitrary",)),
        name="tail",
    )(proj, proj, proj, o1, o2, o3, l1, l2, l3, x2d, sw, sbias, slg, slb, wso, wao, wo, g1, b1, wr, br)


def _row_gather(idx_ref, n, src_hbm, dst, sem):
    def body(r, carry):
        t = idx_ref[0, 0, r]
        pltpu.make_async_copy(src_hbm.at[pl.ds(t, 1)], dst.at[pl.ds(r, 1)], sem).start()
        return carry
    lax.fori_loop(0, n, body, 0, unroll=8)


def _moe_up_kernel(be_ref, tok_ref, tokn_ref, x_hbm, w_ref, b_ref, h_ref, xbuf, sem):
    i = pl.program_id(0)
    n = pl.num_programs(0)
    slot = i % 2

    @pl.when(i == 0)
    def _():
        _row_gather(tok_ref, MOE_TM, x_hbm, xbuf.at[0], sem.at[0])

    @pl.when(i + 1 < n)
    def _():
        _row_gather(tokn_ref, MOE_TM, x_hbm, xbuf.at[1 - slot], sem.at[1 - slot])

    pltpu.make_async_copy(x_hbm.at[pl.ds(0, MOE_TM)], xbuf.at[slot], sem.at[slot]).wait()
    xb = xbuf[slot].astype(BF16)
    nc = PROJ_TN
    for c in range(D_EXPERT // nc):
        cs = slice(c * nc, (c + 1) * nc)
        us = slice(D_EXPERT + c * nc, D_EXPERT + (c + 1) * nc)
        g = jnp.dot(xb, w_ref[0, :, cs], preferred_element_type=F32) + b_ref[0, :, cs]
        up = jnp.dot(xb, w_ref[0, :, us], preferred_element_type=F32) + b_ref[0, :, us]
        g = jnp.minimum(g, SWIGLU_LIMIT)
        up = jnp.clip(up, -SWIGLU_LIMIT, SWIGLU_LIMIT)
        act = (up + 1.0) * (g * (1.0 / (1.0 + jnp.exp(-SWIGLU_ALPHA * g))))
        h_ref[:, cs] = act.astype(BF16)


def _moe_up(block_e, row_tok3, x1, wgu, bgu):
    nblk = block_e.shape[0]
    smem_blk = lambda off: pl.BlockSpec(
        (1, 1, MOE_TM), lambda i, be: (jnp.minimum(i + off, nblk - 1), 0, 0), memory_space=pltpu.SMEM)
    grid_spec = pltpu.PrefetchScalarGridSpec(
        num_scalar_prefetch=1,
        grid=(nblk,),
        in_specs=[smem_blk(0), smem_blk(1),
                  pl.BlockSpec(memory_space=pl.ANY),
                  pl.BlockSpec((1, D_MODEL, 2 * D_EXPERT), lambda i, be: (be[i], 0, 0)),
                  pl.BlockSpec((1, 1, 2 * D_EXPERT), lambda i, be: (be[i], 0, 0))],
        out_specs=pl.BlockSpec((MOE_TM, D_EXPERT), lambda i, be: (i, 0)),
        scratch_shapes=[pltpu.VMEM((2, MOE_TM, D_MODEL), F32), pltpu.SemaphoreType.DMA((2,))],
    )
    return pl.pallas_call(
        _moe_up_kernel,
        grid_spec=grid_spec,
        out_shape=jax.ShapeDtypeStruct((nblk * MOE_TM, D_EXPERT), BF16),
        compiler_params=_cparams(("arbitrary",)),
        name="moe_up",
    )(block_e, row_tok3, row_tok3, x1, wgu, bgu)


def _moe_down_kernel(be_ref, h_ref, w_ref, b_ref, y_ref):
    y_ref[...] = jnp.dot(h_ref[...], w_ref[0], preferred_element_type=F32) + b_ref[0]


def _moe_down(block_e, h, wd, bd):
    nblk = block_e.shape[0]
    grid_spec = pltpu.PrefetchScalarGridSpec(
        num_scalar_prefetch=1,
        grid=(nblk,),
        in_specs=[pl.BlockSpec((MOE_TM, D_EXPERT), lambda i, be: (i, 0)),
                  pl.BlockSpec((1, D_EXPERT, D_MODEL), lambda i, be: (be[i], 0, 0)),
                  pl.BlockSpec((1, 1, D_MODEL), lambda i, be: (be[i], 0, 0))],
        out_specs=pl.BlockSpec((MOE_TM, D_MODEL), lambda i, be: (i, 0)),
    )
    return pl.pallas_call(
        _moe_down_kernel,
        grid_spec=grid_spec,
        out_shape=jax.ShapeDtypeStruct((nblk * MOE_TM, D_MODEL), F32),
        compiler_params=_cparams(("arbitrary",)),
        name="moe_down",
    )(block_e, h, wd, bd)


def _combine_kernel(pos_ref, posn_ref, gate_ref, x1_ref, g2_ref, b2_ref, y_hbm, o_ref, ybuf, sem):
    i = pl.program_id(0)
    n = pl.num_programs(0)
    slot = i % 2
    nrow = TOP_K * COMB_TM

    @pl.when(i == 0)
    def _():
        _row_gather(pos_ref, nrow, y_hbm, ybuf.at[0], sem.at[0])

    @pl.when(i + 1 < n)
    def _():
        _row_gather(posn_ref, nrow, y_hbm, ybuf.at[1 - slot], sem.at[1 - slot])

    pltpu.make_async_copy(y_hbm.at[pl.ds(0, nrow)], ybuf.at[slot], sem.at[slot]).wait()
    gate = gate_ref[...]
    y = jnp.zeros((COMB_TM, D_MODEL), F32)
    for kk in range(TOP_K):
        y = y + gate[:, kk:kk + 1] * ybuf[slot, kk * COMB_TM:(kk + 1) * COMB_TM, :]
    o_ref[...] = _layer_norm(DEEPNORM_ALPHA * x1_ref[...] + y, g2_ref[...], b2_ref[...])


def _combine(pos3, gate, x1, g2, b2, yb):
    S = x1.shape[0]
    nt = S // COMB_TM
    nrow = TOP_K * COMB_TM
    smem_blk = lambda off: pl.BlockSpec(
        (1, 1, nrow), lambda i: (jnp.minimum(i + off, nt - 1), 0, 0), memory_space=pltpu.SMEM)
    return pl.pallas_call(
        _combine_kernel,
        grid=(nt,),
        in_specs=[smem_blk(0), smem_blk(1),
                  pl.BlockSpec((COMB_TM, LANES), lambda i: (i, 0)),
                  pl.BlockSpec((COMB_TM, D_MODEL), lambda i: (i, 0)),
                  pl.BlockSpec((1, D_MODEL), lambda i: (0, 0)),
                  pl.BlockSpec((1, D_MODEL), lambda i: (0, 0)),
                  pl.BlockSpec(memory_space=pl.ANY)],
        out_specs=pl.BlockSpec((COMB_TM, D_MODEL), lambda i: (i, 0)),
        out_shape=jax.ShapeDtypeStruct((S, D_MODEL), F32),
        scratch_shapes=[pltpu.VMEM((2, nrow, D_MODEL), F32), pltpu.SemaphoreType.DMA((2,))],
        compiler_params=_cparams(("arbitrary",)),
        name="combine",
    )(pos3, pos3, gate, x1, g2, b2, yb)


def _rope_tables(pos):
    half = ROT_DIM // 2
    inv_freq = ROPE_THETA ** (-jnp.arange(0, ROT_DIM, 2, dtype=F32) / ROT_DIM)
    ang = pos.astype(F32)[:, None] * inv_freq
    cos, sin = jnp.cos(ang), jnp.sin(ang)
    S = pos.shape[0]
    one = jnp.ones((S, HEAD_DIM - ROT_DIM), F32)
    zero = jnp.zeros((S, HEAD_DIM - ROT_DIM), F32)
    z16 = jnp.zeros((S, half), F32)
    rc = jnp.concatenate([cos, cos, one], axis=1)
    ra = jnp.concatenate([-sin, z16, zero], axis=1)
    rb = jnp.concatenate([z16, sin, zero], axis=1)
    return rc, ra, rb


def _routing(idx, S):
    n = S * TOP_K
    nblk = n // MOE_TM + N_EXPERTS
    onehot = (idx[:, :, None] == jnp.arange(N_EXPERTS, dtype=jnp.int32)[None, None, :])
    member = jnp.sum(onehot, axis=1, dtype=jnp.int32)
    before = jnp.cumsum(member, axis=0) - member
    rank = jnp.take_along_axis(before, idx, axis=1)
    counts = jnp.sum(member, axis=0)
    padded = (counts + MOE_TM - 1) // MOE_TM * MOE_TM
    pends = jnp.cumsum(padded)
    pstarts = pends - padded
    dest = pstarts[idx] + rank
    tok = jnp.broadcast_to(jnp.arange(S, dtype=jnp.int32)[:, None], (S, TOP_K))
    row_tok = jnp.zeros((nblk * MOE_TM,), jnp.int32).at[dest.reshape(-1)].set(tok.reshape(-1))
    block_e = jnp.clip(jnp.searchsorted(pends, jnp.arange(nblk, dtype=jnp.int32) * MOE_TM, side='right'),
                       0, N_EXPERTS - 1).astype(jnp.int32)
    return dest.astype(jnp.int32), row_tok, block_e


def kernel(x, positions, w_in, w_attn_out, sgu_ln_g, sgu_ln_b, sgu_w, sgu_b, w_sgu_out, w_out,
           ln1_g, ln1_b, w_router, b_router, w_gate_up, b_gate_up, w_down, b_down, ln2_g, ln2_b):
    B, S, D = x.shape
    assert B == 1 and D == D_MODEL and w_in.shape[0] == DEPTH
    assert S % (PROJ_TM) == 0 and S % (DIL_PATTERNS[-1][1] * ATTN_BLK) == 0
    x2d = x.reshape(S, D)

    a0, a1, a2 = ATTN_WIDTH, 2 * ATTN_WIDTH, 3 * ATTN_WIDTH
    z1 = a2 + 2 * SGU_WIDTH
    wi = w_in[0]
    w_cat = jnp.concatenate([wi[:, a2:z1], wi[:, z1:z1 + D], wi[:, z1 + D:],
                             wi[:, :a0], wi[:, a0:a1], wi[:, a1:a2]], axis=1).astype(BF16)
    rc, ra, rb = _rope_tables(positions[0])

    proj = _inproj(x2d, w_cat, rc, ra, rb)

    outs, lses = [], []
    for gi, (window, d) in enumerate(DIL_PATTERNS):
        assert window // (2 * d) == ATTN_BLK // 2
        L = S // d
        if d == 1:
            o, l = _attention(proj.reshape(1, S, IN_WIDTH), Q_BLK0 + gi, K_BLK0 + gi, V_BLK0 + gi)
        else:
            cols = [proj[:, (b0 + gi) * GROUP_W:(b0 + gi + 1) * GROUP_W] for b0 in (Q_BLK0, K_BLK0, V_BLK0)]
            qkv = jnp.concatenate(cols, axis=1).reshape(L, d, 3 * GROUP_W).transpose(1, 0, 2)
            o, l = _attention(qkv, 0, 1, 2)
        outs.append(o.transpose(1, 0, 2).reshape(S, GROUP_W))
        lses.append(l.transpose(1, 0, 2).reshape(S, LANES))

    sbias = jnp.repeat(sgu_b[0].T, CHUNK, axis=1)
    wr = jnp.pad(w_router[0], ((0, 0), (0, LANES - N_EXPERTS))).astype(BF16)
    br = jnp.pad(b_router[0], (0, LANES - N_EXPERTS)).reshape(1, LANES)
    x1, idx_f, gate = _tail(
        proj, outs[0], outs[1], outs[2], lses[0], lses[1], lses[2], x2d,
        sgu_w[0].astype(BF16), sbias, sgu_ln_g[0].reshape(1, -1), sgu_ln_b[0].reshape(1, -1),
        w_sgu_out[0].astype(BF16), w_attn_out[0].astype(BF16), w_out[0].astype(BF16),
        ln1_g[0].reshape(1, -1), ln1_b[0].reshape(1, -1), wr, br)

    idx = idx_f[:, :TOP_K].astype(jnp.int32)
    dest, row_tok, block_e = _routing(idx, S)
    nblk = block_e.shape[0]

    h = _moe_up(block_e, row_tok.reshape(nblk, 1, MOE_TM), x1,
                w_gate_up[0].astype(BF16), b_gate_up[0].reshape(N_EXPERTS, 1, -1))
    yb = _moe_down(block_e, h, w_down[0].astype(BF16), b_down[0].reshape(N_EXPERTS, 1, -1))

    nt = S // COMB_TM
    pos3 = dest.reshape(nt, COMB_TM, TOP_K).transpose(0, 2, 1).reshape(nt, 1, TOP_K * COMB_TM)
    out = _combine(pos3, gate, x1, ln2_g[0].reshape(1, -1), ln2_b[0].reshape(1, -1), yb)
    return out.reshape(B, S, D)
```

```python
import numpy as np
import jax
import jax.numpy as jnp
from jax import lax
from jax.experimental import pallas as pl
from jax.experimental.pallas import tpu as pltpu

F32 = jnp.float32
BF16 = jnp.bfloat16

D_MODEL = 2048
HEAD_DIM = 128
ROT_DIM = HEAD_DIM // 4
ROPE_THETA = 500000.0
DIL_PATTERNS = ((128, 1), (512, 4), (2048, 16))
N_GROUPS = len(DIL_PATTERNS)
HEADS = 4
GROUP_W = HEADS * HEAD_DIM
ATTN_WIDTH = N_GROUPS * GROUP_W
CHUNK = 128
SGU_GROUPS = 8
SGU_WIDTH = SGU_GROUPS * CHUNK
IN_WIDTH = 3 * ATTN_WIDTH + 2 * SGU_WIDTH + 2 * D_MODEL
N_EXPERTS = 32
TOP_K = 4
D_EXPERT = D_MODEL
SWIGLU_LIMIT = 7.0
SWIGLU_ALPHA = 1.702
LN_EPS = 1e-5
DEPTH = 1
DEEPNORM_ALPHA = (2 * DEPTH) ** 0.25
NEG_INF = -1e30

LANES = 128
V7X_VMEM_BYTES = 64 * 1024 * 1024
VMEM_LIMIT = V7X_VMEM_BYTES - 8 * 1024 * 1024

COL_BLK = 512
PROJ_TM = 512
PROJ_CHUNKS = 3
ATTN_BLK = 128
ATTN_QB = 4
ATTN_AHEAD = 10
TAIL_TM = 256
MOE_TM = 256
COMB_TM = 128

Z_BLK0, GA_BLK0, GB_BLK0, Q_BLK0, K_BLK0, V_BLK0 = 0, 4, 8, 12, 15, 18
N_PROJ_BLKS = IN_WIDTH // COL_BLK
assert N_PROJ_BLKS == 21 and N_PROJ_BLKS % PROJ_CHUNKS == 0


def _cparams(sem):
    return pltpu.CompilerParams(dimension_semantics=sem, vmem_limit_bytes=VMEM_LIMIT)


def _layer_norm(v, g, b):
    mu = jnp.mean(v, axis=-1, keepdims=True)
    c = v - mu
    var = jnp.mean(c * c, axis=-1, keepdims=True)
    return c * lax.rsqrt(var + LN_EPS) * g + b


def _epilogue_kind(blk):
    if blk < GA_BLK0:
        return "gelu"
    if blk < Q_BLK0:
        return "sigmoid"
    if blk < V_BLK0:
        return "rope"
    return "plain"


def _inproj_kernel(x_ref, w_ref, rc_ref, ra_ref, rb_ref, o_ref, xb_ref):
    j = pl.program_id(1)

    @pl.when(j == 0)
    def _():
        xb_ref[...] = x_ref[...].astype(BF16)

    def epilogue(acc, kind):
        if kind == "gelu":
            return 0.5 * acc * (1.0 + lax.erf(acc * (2.0 ** -0.5)))
        if kind == "sigmoid":
            return 1.0 / (1.0 + jnp.exp(-acc))
        if kind == "rope":
            c = jnp.concatenate([rc_ref[...]] * HEADS, axis=1)
            a = jnp.concatenate([ra_ref[...]] * HEADS, axis=1)
            b = jnp.concatenate([rb_ref[...]] * HEADS, axis=1)
            half = ROT_DIM // 2
            up = pltpu.roll(acc, COL_BLK - half, axis=1)
            dn = pltpu.roll(acc, half, axis=1)
            return acc * c + up * a + dn * b
        return acc

    patterns = {}
    for t in range(N_PROJ_BLKS // PROJ_CHUNKS):
        kinds = tuple(_epilogue_kind(t * PROJ_CHUNKS + c) for c in range(PROJ_CHUNKS))
        patterns.setdefault(kinds, []).append(t)
    for kinds, tiles in patterns.items():
        cond = j == tiles[0]
        for t in tiles[1:]:
            cond = cond | (j == t)

        @pl.when(cond)
        def _(kinds=kinds):
            for c, kind in enumerate(kinds):
                cs = slice(c * COL_BLK, (c + 1) * COL_BLK)
                acc = jnp.dot(xb_ref[...], w_ref[:, cs], preferred_element_type=F32)
                o_ref[:, cs] = epilogue(acc, kind).astype(BF16)


def _inproj(x2d, w_cat, rc, ra, rb):
    S = x2d.shape[0]
    tn = PROJ_CHUNKS * COL_BLK
    return pl.pallas_call(
        _inproj_kernel,
        grid=(S // PROJ_TM, IN_WIDTH // tn),
        in_specs=[
            pl.BlockSpec((PROJ_TM, D_MODEL), lambda i, j: (i, 0)),
            pl.BlockSpec((D_MODEL, tn), lambda i, j: (0, j)),
            pl.BlockSpec((PROJ_TM, LANES), lambda i, j: (i, 0)),
            pl.BlockSpec((PROJ_TM, LANES), lambda i, j: (i, 0)),
            pl.BlockSpec((PROJ_TM, LANES), lambda i, j: (i, 0)),
        ],
        out_specs=pl.BlockSpec((PROJ_TM, tn), lambda i, j: (i, j)),
        out_shape=jax.ShapeDtypeStruct((S, IN_WIDTH), BF16),
        scratch_shapes=[pltpu.VMEM((PROJ_TM, D_MODEL), BF16)],
        compiler_params=_cparams(("arbitrary", "arbitrary")),
        name="inproj",
    )(x2d, w_cat, rc, ra, rb)


def _attn_kernel(q_ref, kp_ref, kc_ref, kn_ref, vp_ref, vc_ref, vn_ref, band_ref, o_ref, l_ref):
    i = pl.program_id(1)
    n = pl.num_programs(1)
    k = jnp.concatenate([kp_ref[0], kc_ref[0], kn_ref[0]], axis=0)
    v = jnp.concatenate([vp_ref[0], vc_ref[0], vn_ref[0]], axis=0)
    hq = ATTN_BLK // 2
    col = lax.broadcasted_iota(jnp.int32, (hq, 2 * ATTN_BLK), 1)
    lane = lax.broadcasted_iota(jnp.int32, (hq, LANES), 1)
    scale = HEAD_DIM ** -0.5
    first_bias = jnp.where(col >= jnp.where(i > 0, 0, ATTN_BLK), band_ref[0], NEG_INF)
    last_bias = jnp.where(col < jnp.where(i < n - 1, 2 * ATTN_BLK, ATTN_BLK), band_ref[1], NEG_INF)

    units = [(s, hf, h) for s in range(ATTN_QB) for hf in range(2) for h in range(HEADS)]

    def scores(s, hf, h):
        qs = slice(s * ATTN_BLK + hf * hq, s * ATTN_BLK + (hf + 1) * hq)
        ks = slice((s + hf) * ATTN_BLK, (s + hf + 2) * ATTN_BLK)
        hs = slice(h * HEAD_DIM, (h + 1) * HEAD_DIM)
        return lax.dot_general(q_ref[0, qs, hs], k[ks, hs], (((1,), (1,)), ((), ())),
                               preferred_element_type=F32)

    pending = [scores(*u) for u in units[:ATTN_AHEAD]]
    lse_tile = None
    for j, (s, hf, h) in enumerate(units):
        raw = pending.pop(0)
        if j + ATTN_AHEAD < len(units):
            pending.append(scores(*units[j + ATTN_AHEAD]))
        if s == 0 and hf == 0:
            bias = first_bias
        elif s == ATTN_QB - 1 and hf == 1:
            bias = last_bias
        else:
            bias = band_ref[hf]
        qs = slice(s * ATTN_BLK + hf * hq, s * ATTN_BLK + (hf + 1) * hq)
        ks = slice((s + hf) * ATTN_BLK, (s + hf + 2) * ATTN_BLK)
        hs = slice(h * HEAD_DIM, (h + 1) * HEAD_DIM)
        sc = raw * scale + bias
        m = jnp.max(sc, axis=-1, keepdims=True)
        p = jnp.exp(sc - m)
        den = jnp.sum(p, axis=-1, keepdims=True)
        o = jnp.dot(p.astype(BF16), v[ks, hs], preferred_element_type=F32) / den
        o_ref[0, qs, hs] = o.astype(BF16)
        if h == 0:
            lse_tile = jnp.zeros((hq, LANES), F32)
        lse_tile = jnp.where(lane == h, m + jnp.log(den), lse_tile)
        if h == HEADS - 1:
            l_ref[0, qs, :] = lse_tile


def _band_bias():
    hq = ATTN_BLK // 2
    out = np.empty((2, hq, 2 * ATTN_BLK), np.float32)
    for hf in range(2):
        i = np.arange(hq)[:, None] + hf * hq
        c = np.arange(2 * ATTN_BLK)[None, :] + hf * ATTN_BLK
        rel = c - ATTN_BLK - i
        out[hf] = np.where(np.abs(rel) <= hq, 0.0, NEG_INF)
    return jnp.asarray(out)


def _attention(arr, qb, kb, vb):
    d, L, _ = arr.shape
    assert ATTN_QB >= 2 and L % (ATTN_QB * ATTN_BLK) == 0
    nb = L // ATTN_BLK
    big = (1, ATTN_QB * ATTN_BLK, GROUP_W)
    small = (1, ATTN_BLK, GROUP_W)
    cur = lambda cb: pl.BlockSpec(big, lambda r, i: (r, i, cb))
    prev = lambda cb: pl.BlockSpec(small, lambda r, i: (r, jnp.maximum(ATTN_QB * i - 1, 0), cb))
    nxt = lambda cb: pl.BlockSpec(small, lambda r, i: (r, jnp.minimum(ATTN_QB * (i + 1), nb - 1), cb))
    return pl.pallas_call(
        _attn_kernel,
        grid=(d, nb // ATTN_QB),
        in_specs=[cur(qb), prev(kb), cur(kb), nxt(kb), prev(vb), cur(vb), nxt(vb),
                  pl.BlockSpec((2, ATTN_BLK // 2, 2 * ATTN_BLK), lambda r, i: (0, 0, 0))],
        out_specs=[pl.BlockSpec(big, lambda r, i: (r, i, 0)),
                   pl.BlockSpec((1, ATTN_QB * ATTN_BLK, LANES), lambda r, i: (r, i, 0))],
        out_shape=[jax.ShapeDtypeStruct((d, L, GROUP_W), BF16),
                   jax.ShapeDtypeStruct((d, L, LANES), F32)],
        compiler_params=_cparams(("arbitrary", "arbitrary")),
        name=f"attn_d{d}",
    )(arr, arr, arr, arr, arr, arr, arr, _band_bias())


def _tail_kernel(z_ref, ga_ref, gb_ref, o1_ref, o2_ref, o3_ref, l1_ref, l2_ref, l3_ref, x_ref,
                 sw_ref, sbias_ref, slg_ref, slb_ref, wso_ref, wao_ref, wo_ref,
                 g1_ref, b1_ref, wr_ref, br_ref,
                 x1_ref, idx_ref, gate_ref):
    tm = z_ref.shape[0]
    z = z_ref[...]
    u = z[:, :SGU_WIDTH].astype(F32)
    vn = _layer_norm(z[:, SGU_WIDTH:].astype(F32), slg_ref[...], slb_ref[...]).astype(BF16)
    rows = []
    for c in range(tm // CHUNK):
        cols = []
        for g in range(SGU_GROUPS):
            blk = vn[c * CHUNK:(c + 1) * CHUNK, g * CHUNK:(g + 1) * CHUNK]
            cols.append(jnp.dot(sw_ref[g], blk, preferred_element_type=F32))
        rows.append(jnp.concatenate(cols, axis=1) + sbias_ref[...])
    mixed = jnp.concatenate(rows, axis=0)
    y_b = jnp.dot((u * mixed).astype(BF16), wso_ref[...], preferred_element_type=F32)

    l1, l2, l3 = l1_ref[...], l2_ref[...], l3_ref[...]
    m = jnp.maximum(jnp.maximum(l1, l2), l3)
    e1, e2, e3 = jnp.exp(l1 - m), jnp.exp(l2 - m), jnp.exp(l3 - m)
    inv = 1.0 / (e1 + e2 + e3)
    w1, w2, w3 = e1 * inv, e2 * inv, e3 * inv
    parts = []
    for h in range(HEADS):
        hs = slice(h * HEAD_DIM, (h + 1) * HEAD_DIM)
        parts.append(w1[:, h:h + 1] * o1_ref[:, hs].astype(F32)
                     + w2[:, h:h + 1] * o2_ref[:, hs].astype(F32)
                     + w3[:, h:h + 1] * o3_ref[:, hs].astype(F32))
    o_a = jnp.concatenate(parts, axis=1).astype(BF16)
    y_a = jnp.dot(o_a, wao_ref[...], preferred_element_type=F32)

    merged = ga_ref[...].astype(F32) * y_a + gb_ref[...].astype(F32) * y_b
    hmix = jnp.dot(merged.astype(BF16), wo_ref[...], preferred_element_type=F32)
    x1 = _layer_norm(DEEPNORM_ALPHA * x_ref[...] + hmix, g1_ref[...], b1_ref[...])
    x1_ref[...] = x1

    logits = jnp.dot(x1.astype(BF16), wr_ref[...], preferred_element_type=F32) + br_ref[...]
    lane = lax.broadcasted_iota(jnp.int32, (tm, LANES), 1)
    lane_f = lane.astype(F32)
    vals = jnp.where(lane < N_EXPERTS, logits, -jnp.inf)
    idx_tile = jnp.zeros((tm, LANES), F32)
    val_tile = jnp.zeros((tm, LANES), F32)
    top0 = None
    for kk in range(TOP_K):
        mk = jnp.max(vals, axis=-1, keepdims=True)
        ik = jnp.min(jnp.where(vals == mk, lane_f, float(LANES)), axis=-1, keepdims=True)
        vals = jnp.where(lane_f == ik, -jnp.inf, vals)
        if kk == 0:
            top0 = mk
        idx_tile = jnp.where(lane == kk, ik, idx_tile)
        val_tile = jnp.where(lane == kk, jnp.exp(mk - top0), val_tile)
    idx_ref[...] = idx_tile
    gate_ref[...] = val_tile / jnp.sum(val_tile, axis=-1, keepdims=True)


def _tail(proj, o1, o2, o3, l1, l2, l3, x2d, sw, sbias, slg, slb, wso, wao, wo, g1, b1, wr, br):
    S = x2d.shape[0]
    tm = TAIL_TM
    row = lambda w: pl.BlockSpec((tm, w), lambda i: (i, 0))

    def const(shape):
        nd = len(shape)
        return pl.BlockSpec(shape, lambda i: (0,) * nd, pipeline_mode=pl.Buffered(1))

    wide = lambda cb: pl.BlockSpec((tm, D_MODEL), lambda i: (i, cb))
    return pl.pallas_call(
        _tail_kernel,
        grid=(S // tm,),
        in_specs=[wide(0), wide(1), wide(2),
                  row(GROUP_W), row(GROUP_W), row(GROUP_W), row(LANES), row(LANES), row(LANES),
                  row(D_MODEL),
                  const(sw.shape), const(sbias.shape), const(slg.shape), const(slb.shape),
                  const(wso.shape), const(wao.shape), const(wo.shape),
                  const(g1.shape), const(b1.shape), const(wr.shape), const(br.shape)],
        out_specs=[row(D_MODEL), row(LANES), row(LANES)],
        out_shape=[jax.ShapeDtypeStruct((S, D_MODEL), F32),
                   jax.ShapeDtypeStruct((S, LANES), F32),
                   jax.ShapeDtypeStruct((S, LANES), F32)],
        compiler_params=_cparams(("arbitrary",)),
        name="tail",
    )(proj, proj, proj, o1, o2, o3, l1, l2, l3, x2d, sw, sbias, slg, slb, wso, wao, wo, g1, b1, wr, br)


def _row_gather(idx_ref, n, src_hbm, dst, sem):
    def body(r, carry):
        t = idx_ref[0, 0, r]
        pltpu.make_async_copy(src_hbm.at[pl.ds(t, 1)], dst.at[pl.ds(r, 1)], sem).start()
        return carry
    lax.fori_loop(0, n, body, 0, unroll=8)


def _moe_up_kernel(be_ref, nu_ref, tok_ref, tokn_ref, x_hbm, w_ref, b_ref, h_ref, xbuf, sem):
    i = pl.program_id(0)
    nused = nu_ref[0]
    slot = i % 2

    @pl.when(i == 0)
    def _():
        _row_gather(tok_ref, MOE_TM, x_hbm, xbuf.at[0], sem.at[0])

    @pl.when(i + 1 < nused)
    def _():
        _row_gather(tokn_ref, MOE_TM, x_hbm, xbuf.at[1 - slot], sem.at[1 - slot])

    @pl.when(i < nused)
    def _():
        pltpu.make_async_copy(x_hbm.at[pl.ds(0, MOE_TM)], xbuf.at[slot], sem.at[slot]).wait()
        xb = xbuf[slot].astype(BF16)
        for c in range(D_EXPERT // COL_BLK):
            cs = slice(c * COL_BLK, (c + 1) * COL_BLK)
            us = slice(D_EXPERT + c * COL_BLK, D_EXPERT + (c + 1) * COL_BLK)
            g = jnp.dot(xb, w_ref[0, :, cs], preferred_element_type=F32) + b_ref[0, :, cs]
            up = jnp.dot(xb, w_ref[0, :, us], preferred_element_type=F32) + b_ref[0, :, us]
            g = jnp.minimum(g, SWIGLU_LIMIT)
            up = jnp.clip(up, -SWIGLU_LIMIT, SWIGLU_LIMIT)
            act = (up + 1.0) * (g * (1.0 / (1.0 + jnp.exp(-SWIGLU_ALPHA * g))))
            h_ref[:, cs] = act.astype(BF16)

    @pl.when(i >= nused)
    def _():
        h_ref[...] = jnp.zeros(h_ref.shape, h_ref.dtype)


def _moe_up(block_e, nused, row_tok3, x1, wgu, bgu):
    nblk = block_e.shape[0]
    smem_blk = lambda off: pl.BlockSpec(
        (1, 1, MOE_TM), lambda i, be, nu: (jnp.minimum(i + off, nblk - 1), 0, 0), memory_space=pltpu.SMEM)
    grid_spec = pltpu.PrefetchScalarGridSpec(
        num_scalar_prefetch=2,
        grid=(nblk,),
        in_specs=[smem_blk(0), smem_blk(1),
                  pl.BlockSpec(memory_space=pl.ANY),
                  pl.BlockSpec((1, D_MODEL, 2 * D_EXPERT), lambda i, be, nu: (be[i], 0, 0)),
                  pl.BlockSpec((1, 1, 2 * D_EXPERT), lambda i, be, nu: (be[i], 0, 0))],
        out_specs=pl.BlockSpec((MOE_TM, D_EXPERT), lambda i, be, nu: (i, 0)),
        scratch_shapes=[pltpu.VMEM((2, MOE_TM, D_MODEL), F32), pltpu.SemaphoreType.DMA((2,))],
    )
    return pl.pallas_call(
        _moe_up_kernel,
        grid_spec=grid_spec,
        out_shape=jax.ShapeDtypeStruct((nblk * MOE_TM, D_EXPERT), BF16),
        compiler_params=_cparams(("arbitrary",)),
        name="moe_up",
    )(block_e, nused, row_tok3, row_tok3, x1, wgu, bgu)


def _moe_down_kernel(be_ref, nu_ref, h_ref, w_ref, b_ref, y_ref):
    i = pl.program_id(0)

    @pl.when(i < nu_ref[0])
    def _():
        y_ref[...] = jnp.dot(h_ref[...], w_ref[0], preferred_element_type=F32) + b_ref[0]

    @pl.when(i >= nu_ref[0])
    def _():
        y_ref[...] = jnp.zeros(y_ref.shape, y_ref.dtype)


def _moe_down(block_e, nused, h, wd, bd):
    nblk = block_e.shape[0]
    grid_spec = pltpu.PrefetchScalarGridSpec(
        num_scalar_prefetch=2,
        grid=(nblk,),
        in_specs=[pl.BlockSpec((MOE_TM, D_EXPERT), lambda i, be, nu: (jnp.minimum(i, nu[0] - 1), 0)),
                  pl.BlockSpec((1, D_EXPERT, D_MODEL), lambda i, be, nu: (be[i], 0, 0)),
                  pl.BlockSpec((1, 1, D_MODEL), lambda i, be, nu: (be[i], 0, 0))],
        out_specs=pl.BlockSpec((MOE_TM, D_MODEL), lambda i, be, nu: (i, 0)),
    )
    return pl.pallas_call(
        _moe_down_kernel,
        grid_spec=grid_spec,
        out_shape=jax.ShapeDtypeStruct((nblk * MOE_TM, D_MODEL), F32),
        compiler_params=_cparams(("arbitrary",)),
        name="moe_down",
    )(block_e, nused, h, wd, bd)


def _combine_kernel(pos_ref, posn_ref, gate_ref, x1_ref, g2_ref, b2_ref, y_hbm, o_ref, ybuf, sem):
    i = pl.program_id(0)
    n = pl.num_programs(0)
    slot = i % 2
    nrow = TOP_K * COMB_TM

    @pl.when(i == 0)
    def _():
        _row_gather(pos_ref, nrow, y_hbm, ybuf.at[0], sem.at[0])

    @pl.when(i + 1 < n)
    def _():
        _row_gather(posn_ref, nrow, y_hbm, ybuf.at[1 - slot], sem.at[1 - slot])

    pltpu.make_async_copy(y_hbm.at[pl.ds(0, nrow)], ybuf.at[slot], sem.at[slot]).wait()
    gate = gate_ref[...]
    y = jnp.zeros((COMB_TM, D_MODEL), F32)
    for kk in range(TOP_K):
        y = y + gate[:, kk:kk + 1] * ybuf[slot, kk * COMB_TM:(kk + 1) * COMB_TM, :]
    o_ref[...] = _layer_norm(DEEPNORM_ALPHA * x1_ref[...] + y, g2_ref[...], b2_ref[...])


def _combine(pos3, gate, x1, g2, b2, yb):
    S = x1.shape[0]
    nt = S // COMB_TM
    nrow = TOP_K * COMB_TM
    smem_blk = lambda off: pl.BlockSpec(
        (1, 1, nrow), lambda i: (jnp.minimum(i + off, nt - 1), 0, 0), memory_space=pltpu.SMEM)
    return pl.pallas_call(
        _combine_kernel,
        grid=(nt,),
        in_specs=[smem_blk(0), smem_blk(1),
                  pl.BlockSpec((COMB_TM, LANES), lambda i: (i, 0)),
                  pl.BlockSpec((COMB_TM, D_MODEL), lambda i: (i, 0)),
                  pl.BlockSpec((1, D_MODEL), lambda i: (0, 0)),
                  pl.BlockSpec((1, D_MODEL), lambda i: (0, 0)),
                  pl.BlockSpec(memory_space=pl.ANY)],
        out_specs=pl.BlockSpec((COMB_TM, D_MODEL), lambda i: (i, 0)),
        out_shape=jax.ShapeDtypeStruct((S, D_MODEL), F32),
        scratch_shapes=[pltpu.VMEM((2, nrow, D_MODEL), F32), pltpu.SemaphoreType.DMA((2,))],
        compiler_params=_cparams(("arbitrary",)),
        name="combine",
    )(pos3, pos3, gate, x1, g2, b2, yb)


def _rope_tables(pos):
    half = ROT_DIM // 2
    inv_freq = ROPE_THETA ** (-jnp.arange(0, ROT_DIM, 2, dtype=F32) / ROT_DIM)
    ang = pos.astype(F32)[:, None] * inv_freq
    cos, sin = jnp.cos(ang), jnp.sin(ang)
    S = pos.shape[0]
    one = jnp.ones((S, HEAD_DIM - ROT_DIM), F32)
    zero = jnp.zeros((S, HEAD_DIM - ROT_DIM), F32)
    z16 = jnp.zeros((S, half), F32)
    rc = jnp.concatenate([cos, cos, one], axis=1)
    ra = jnp.concatenate([-sin, z16, zero], axis=1)
    rb = jnp.concatenate([z16, sin, zero], axis=1)
    return rc, ra, rb


def _routing(idx, S):
    n = S * TOP_K
    nblk = n // MOE_TM + N_EXPERTS
    onehot = (idx[:, :, None] == jnp.arange(N_EXPERTS, dtype=jnp.int32)[None, None, :])
    member = jnp.sum(onehot, axis=1, dtype=jnp.int32)
    before = jnp.cumsum(member, axis=0) - member
    rank = jnp.take_along_axis(before, idx, axis=1)
    counts = jnp.sum(member, axis=0)
    padded = (counts + MOE_TM - 1) // MOE_TM * MOE_TM
    pends = jnp.cumsum(padded)
    pstarts = pends - padded
    dest = pstarts[idx] + rank
    tok = jnp.broadcast_to(jnp.arange(S, dtype=jnp.int32)[:, None], (S, TOP_K))
    row_tok = jnp.zeros((nblk * MOE_TM,), jnp.int32).at[dest.reshape(-1)].set(tok.reshape(-1))
    starts = jnp.arange(nblk, dtype=jnp.int32) * MOE_TM
    block_e = jnp.minimum(jnp.sum(pends[None, :] <= starts[:, None], axis=1, dtype=jnp.int32), N_EXPERTS - 1)
    nused = (pends[-1:] // MOE_TM).astype(jnp.int32)
    return dest.astype(jnp.int32), row_tok, block_e, nused


def kernel(x, positions, w_in, w_attn_out, sgu_ln_g, sgu_ln_b, sgu_w, sgu_b, w_sgu_out, w_out,
           ln1_g, ln1_b, w_router, b_router, w_gate_up, b_gate_up, w_down, b_down, ln2_g, ln2_b):
    B, S, D = x.shape
    assert B == 1 and D == D_MODEL and w_in.shape[0] == DEPTH
    assert S % PROJ_TM == 0 and S % (DIL_PATTERNS[-1][1] * ATTN_QB * ATTN_BLK) == 0
    x2d = x.reshape(S, D)

    a0, a1, a2 = ATTN_WIDTH, 2 * ATTN_WIDTH, 3 * ATTN_WIDTH
    z1 = a2 + 2 * SGU_WIDTH
    wi = w_in[0]
    w_cat = jnp.concatenate([wi[:, a2:z1], wi[:, z1:z1 + D], wi[:, z1 + D:],
                             wi[:, :a0], wi[:, a0:a1], wi[:, a1:a2]], axis=1).astype(BF16)
    rc, ra, rb = _rope_tables(positions[0])

    proj = _inproj(x2d, w_cat, rc, ra, rb)

    outs, lses = [], []
    for gi, (window, d) in enumerate(DIL_PATTERNS):
        assert window // (2 * d) == ATTN_BLK // 2
        L = S // d
        if d == 1:
            o, l = _attention(proj.reshape(1, S, IN_WIDTH), Q_BLK0 + gi, K_BLK0 + gi, V_BLK0 + gi)
        else:
            cols = [proj[:, (b0 + gi) * GROUP_W:(b0 + gi + 1) * GROUP_W] for b0 in (Q_BLK0, K_BLK0, V_BLK0)]
            qkv = jnp.concatenate(cols, axis=1).reshape(L, d, 3 * GROUP_W).transpose(1, 0, 2)
            o, l = _attention(qkv, 0, 1, 2)
        outs.append(o.transpose(1, 0, 2).reshape(S, GROUP_W))
        lses.append(l.transpose(1, 0, 2).reshape(S, LANES))

    sbias = jnp.repeat(sgu_b[0].T, CHUNK, axis=1)
    wr = jnp.pad(w_router[0], ((0, 0), (0, LANES - N_EXPERTS))).astype(BF16)
    br = jnp.pad(b_router[0], (0, LANES - N_EXPERTS)).reshape(1, LANES)
    x1, idx_f, gate = _tail(
        proj, outs[0], outs[1], outs[2], lses[0], lses[1], lses[2], x2d,
        sgu_w[0].astype(BF16), sbias, sgu_ln_g[0].reshape(1, -1), sgu_ln_b[0].reshape(1, -1),
        w_sgu_out[0].astype(BF16), w_attn_out[0].astype(BF16), w_out[0].astype(BF16),
        ln1_g[0].reshape(1, -1), ln1_b[0].reshape(1, -1), wr, br)

    idx = idx_f[:, :TOP_K].astype(jnp.int32)
    dest, row_tok, block_e, nused = _routing(idx, S)
    nblk = block_e.shape[0]

    h = _moe_up(block_e, nused, row_tok.reshape(nblk, 1, MOE_TM), x1,
                w_gate_up[0].astype(BF16), b_gate_up[0].reshape(N_EXPERTS, 1, -1))
    yb = _moe_down(block_e, nused, h, w_down[0].astype(BF16), b_down[0].reshape(N_EXPERTS, 1, -1))

    nt = S // COMB_TM
    pos3 = dest.reshape(nt, COMB_TM, TOP_K).transpose(0, 2, 1).reshape(nt, 1, TOP_K * COMB_TM)
    out = _combine(pos3, gate, x1, ln2_g[0].reshape(1, -1), ln2_b[0].reshape(1, -1), yb)
    return out.reshape(B, S, D)
```

```python
import numpy as np
import jax
import jax.numpy as jnp
from jax import lax
from jax.experimental import pallas as pl
from jax.experimental.pallas import tpu as pltpu

F32 = jnp.float32
BF16 = jnp.bfloat16

D_MODEL = 2048
HEAD_DIM = 128
ROT_DIM = HEAD_DIM // 4
ROPE_THETA = 500000.0
DIL_PATTERNS = ((128, 1), (512, 4), (2048, 16))
N_GROUPS = len(DIL_PATTERNS)
HEADS = 4
GROUP_W = HEADS * HEAD_DIM
ATTN_WIDTH = N_GROUPS * GROUP_W
CHUNK = 128
SGU_GROUPS = 8
SGU_WIDTH = SGU_GROUPS * CHUNK
IN_WIDTH = 3 * ATTN_WIDTH + 2 * SGU_WIDTH + 2 * D_MODEL
N_EXPERTS = 32
TOP_K = 4
D_EXPERT = D_MODEL
SWIGLU_LIMIT = 7.0
SWIGLU_ALPHA = 1.702
LN_EPS = 1e-5
DEPTH = 1
DEEPNORM_ALPHA = (2 * DEPTH) ** 0.25
NEG_INF = -1e30

LANES = 128
V7X_VMEM_BYTES = 64 * 1024 * 1024
VMEM_LIMIT = V7X_VMEM_BYTES - 8 * 1024 * 1024

COL_BLK = 512
PROJ_TM = 512
PROJ_CHUNKS = 3
ATTN_BLK = 128
ATTN_QB = 4
ATTN_AHEAD = 10
TAIL_TM = 256
MOE_TM = 256
W_CHUNKS = 8
COMB_TM = 128

Z_BLK0, GA_BLK0, GB_BLK0, Q_BLK0, K_BLK0, V_BLK0 = 0, 4, 8, 12, 15, 18
N_PROJ_BLKS = IN_WIDTH // COL_BLK
assert N_PROJ_BLKS == 21 and N_PROJ_BLKS % PROJ_CHUNKS == 0


def _cparams(sem):
    return pltpu.CompilerParams(dimension_semantics=sem, vmem_limit_bytes=VMEM_LIMIT)


def _layer_norm(v, g, b):
    mu = jnp.mean(v, axis=-1, keepdims=True)
    c = v - mu
    var = jnp.mean(c * c, axis=-1, keepdims=True)
    return c * lax.rsqrt(var + LN_EPS) * g + b


def _epilogue_kind(blk):
    if blk < GA_BLK0:
        return "gelu"
    if blk < Q_BLK0:
        return "sigmoid"
    if blk < V_BLK0:
        return "rope"
    return "plain"


def _inproj_kernel(x_ref, w_ref, rc_ref, ra_ref, rb_ref, o_ref, xb_ref):
    j = pl.program_id(1)

    @pl.when(j == 0)
    def _():
        xb_ref[...] = x_ref[...].astype(BF16)

    def epilogue(acc, kind):
        if kind == "gelu":
            return 0.5 * acc * (1.0 + lax.erf(acc * (2.0 ** -0.5)))
        if kind == "sigmoid":
            return 1.0 / (1.0 + jnp.exp(-acc))
        if kind == "rope":
            c = jnp.concatenate([rc_ref[...]] * HEADS, axis=1)
            a = jnp.concatenate([ra_ref[...]] * HEADS, axis=1)
            b = jnp.concatenate([rb_ref[...]] * HEADS, axis=1)
            half = ROT_DIM // 2
            up = pltpu.roll(acc, COL_BLK - half, axis=1)
            dn = pltpu.roll(acc, half, axis=1)
            return acc * c + up * a + dn * b
        return acc

    patterns = {}
    for t in range(N_PROJ_BLKS // PROJ_CHUNKS):
        kinds = tuple(_epilogue_kind(t * PROJ_CHUNKS + c) for c in range(PROJ_CHUNKS))
        patterns.setdefault(kinds, []).append(t)
    for kinds, tiles in patterns.items():
        cond = j == tiles[0]
        for t in tiles[1:]:
            cond = cond | (j == t)

        @pl.when(cond)
        def _(kinds=kinds):
            for c, kind in enumerate(kinds):
                cs = slice(c * COL_BLK, (c + 1) * COL_BLK)
                acc = jnp.dot(xb_ref[...], w_ref[:, cs], preferred_element_type=F32)
                o_ref[:, cs] = epilogue(acc, kind).astype(BF16)


def _inproj(x2d, w_cat, rc, ra, rb):
    S = x2d.shape[0]
    tn = PROJ_CHUNKS * COL_BLK
    return pl.pallas_call(
        _inproj_kernel,
        grid=(S // PROJ_TM, IN_WIDTH // tn),
        in_specs=[
            pl.BlockSpec((PROJ_TM, D_MODEL), lambda i, j: (i, 0)),
            pl.BlockSpec((D_MODEL, tn), lambda i, j: (0, j)),
            pl.BlockSpec((PROJ_TM, LANES), lambda i, j: (i, 0)),
            pl.BlockSpec((PROJ_TM, LANES), lambda i, j: (i, 0)),
            pl.BlockSpec((PROJ_TM, LANES), lambda i, j: (i, 0)),
        ],
        out_specs=pl.BlockSpec((PROJ_TM, tn), lambda i, j: (i, j)),
        out_shape=jax.ShapeDtypeStruct((S, IN_WIDTH), BF16),
        scratch_shapes=[pltpu.VMEM((PROJ_TM, D_MODEL), BF16)],
        compiler_params=_cparams(("arbitrary", "arbitrary")),
        name="inproj",
    )(x2d, w_cat, rc, ra, rb)


def _attn_kernel(q_ref, kp_ref, kc_ref, kn_ref, vp_ref, vc_ref, vn_ref, band_ref, o_ref, l_ref):
    i = pl.program_id(1)
    n = pl.num_programs(1)
    k = jnp.concatenate([kp_ref[0], kc_ref[0], kn_ref[0]], axis=0)
    v = jnp.concatenate([vp_ref[0], vc_ref[0], vn_ref[0]], axis=0)
    hq = ATTN_BLK // 2
    col = lax.broadcasted_iota(jnp.int32, (hq, 2 * ATTN_BLK), 1)
    lane = lax.broadcasted_iota(jnp.int32, (hq, LANES), 1)
    scale = HEAD_DIM ** -0.5
    first_bias = jnp.where(col >= jnp.where(i > 0, 0, ATTN_BLK), band_ref[0], NEG_INF)
    last_bias = jnp.where(col < jnp.where(i < n - 1, 2 * ATTN_BLK, ATTN_BLK), band_ref[1], NEG_INF)

    units = [(s, hf, h) for s in range(ATTN_QB) for hf in range(2) for h in range(HEADS)]

    def scores(s, hf, h):
        qs = slice(s * ATTN_BLK + hf * hq, s * ATTN_BLK + (hf + 1) * hq)
        ks = slice((s + hf) * ATTN_BLK, (s + hf + 2) * ATTN_BLK)
        hs = slice(h * HEAD_DIM, (h + 1) * HEAD_DIM)
        return lax.dot_general(q_ref[0, qs, hs], k[ks, hs], (((1,), (1,)), ((), ())),
                               preferred_element_type=F32)

    pending = [scores(*u) for u in units[:ATTN_AHEAD]]
    lse_tile = None
    for j, (s, hf, h) in enumerate(units):
        raw = pending.pop(0)
        if j + ATTN_AHEAD < len(units):
            pending.append(scores(*units[j + ATTN_AHEAD]))
        if s == 0 and hf == 0:
            bias = first_bias
        elif s == ATTN_QB - 1 and hf == 1:
            bias = last_bias
        else:
            bias = band_ref[hf]
        qs = slice(s * ATTN_BLK + hf * hq, s * ATTN_BLK + (hf + 1) * hq)
        ks = slice((s + hf) * ATTN_BLK, (s + hf + 2) * ATTN_BLK)
        hs = slice(h * HEAD_DIM, (h + 1) * HEAD_DIM)
        sc = raw * scale + bias
        m = jnp.max(sc, axis=-1, keepdims=True)
        p = jnp.exp(sc - m)
        den = jnp.sum(p, axis=-1, keepdims=True)
        o = jnp.dot(p.astype(BF16), v[ks, hs], preferred_element_type=F32) / den
        o_ref[0, qs, hs] = o.astype(BF16)
        if h == 0:
            lse_tile = jnp.zeros((hq, LANES), F32)
        lse_tile = jnp.where(lane == h, m + jnp.log(den), lse_tile)
        if h == HEADS - 1:
            l_ref[0, qs, :] = lse_tile


def _band_bias():
    hq = ATTN_BLK // 2
    out = np.empty((2, hq, 2 * ATTN_BLK), np.float32)
    for hf in range(2):
        i = np.arange(hq)[:, None] + hf * hq
        c = np.arange(2 * ATTN_BLK)[None, :] + hf * ATTN_BLK
        rel = c - ATTN_BLK - i
        out[hf] = np.where(np.abs(rel) <= hq, 0.0, NEG_INF)
    return jnp.asarray(out)


def _attention(arr, qb, kb, vb):
    d, L, _ = arr.shape
    assert ATTN_QB >= 2 and L % (ATTN_QB * ATTN_BLK) == 0
    nb = L // ATTN_BLK
    big = (1, ATTN_QB * ATTN_BLK, GROUP_W)
    small = (1, ATTN_BLK, GROUP_W)
    cur = lambda cb: pl.BlockSpec(big, lambda r, i: (r, i, cb))
    prev = lambda cb: pl.BlockSpec(small, lambda r, i: (r, jnp.maximum(ATTN_QB * i - 1, 0), cb))
    nxt = lambda cb: pl.BlockSpec(small, lambda r, i: (r, jnp.minimum(ATTN_QB * (i + 1), nb - 1), cb))
    return pl.pallas_call(
        _attn_kernel,
        grid=(d, nb // ATTN_QB),
        in_specs=[cur(qb), prev(kb), cur(kb), nxt(kb), prev(vb), cur(vb), nxt(vb),
                  pl.BlockSpec((2, ATTN_BLK // 2, 2 * ATTN_BLK), lambda r, i: (0, 0, 0))],
        out_specs=[pl.BlockSpec(big, lambda r, i: (r, i, 0)),
                   pl.BlockSpec((1, ATTN_QB * ATTN_BLK, LANES), lambda r, i: (r, i, 0))],
        out_shape=[jax.ShapeDtypeStruct((d, L, GROUP_W), BF16),
                   jax.ShapeDtypeStruct((d, L, LANES), F32)],
        compiler_params=_cparams(("arbitrary", "arbitrary")),
        name=f"attn_d{d}",
    )(arr, arr, arr, arr, arr, arr, arr, _band_bias())


def _tail_kernel(z_ref, ga_ref, gb_ref, o1_ref, o2_ref, o3_ref, l1_ref, l2_ref, l3_ref, x_ref,
                 sw_ref, sbias_ref, slg_ref, slb_ref, wso_ref, wao_ref, wo_ref,
                 g1_ref, b1_ref, wr_ref, br_ref,
                 x1_ref, idx_ref, gate_ref):
    tm = z_ref.shape[0]
    z = z_ref[...]
    u = z[:, :SGU_WIDTH].astype(F32)
    vn = _layer_norm(z[:, SGU_WIDTH:].astype(F32), slg_ref[...], slb_ref[...]).astype(BF16)
    rows = []
    for c in range(tm // CHUNK):
        cols = []
        for g in range(SGU_GROUPS):
            blk = vn[c * CHUNK:(c + 1) * CHUNK, g * CHUNK:(g + 1) * CHUNK]
            cols.append(jnp.dot(sw_ref[g], blk, preferred_element_type=F32))
        rows.append(jnp.concatenate(cols, axis=1) + sbias_ref[...])
    mixed = jnp.concatenate(rows, axis=0)
    y_b = jnp.dot((u * mixed).astype(BF16), wso_ref[...], preferred_element_type=F32)

    l1, l2, l3 = l1_ref[...], l2_ref[...], l3_ref[...]
    m = jnp.maximum(jnp.maximum(l1, l2), l3)
    e1, e2, e3 = jnp.exp(l1 - m), jnp.exp(l2 - m), jnp.exp(l3 - m)
    inv = 1.0 / (e1 + e2 + e3)
    w1, w2, w3 = e1 * inv, e2 * inv, e3 * inv
    parts = []
    for h in range(HEADS):
        hs = slice(h * HEAD_DIM, (h + 1) * HEAD_DIM)
        parts.append(w1[:, h:h + 1] * o1_ref[:, hs].astype(F32)
                     + w2[:, h:h + 1] * o2_ref[:, hs].astype(F32)
                     + w3[:, h:h + 1] * o3_ref[:, hs].astype(F32))
    o_a = jnp.concatenate(parts, axis=1).astype(BF16)
    y_a = jnp.dot(o_a, wao_ref[...], preferred_element_type=F32)

    merged = ga_ref[...].astype(F32) * y_a + gb_ref[...].astype(F32) * y_b
    hmix = jnp.dot(merged.astype(BF16), wo_ref[...], preferred_element_type=F32)
    x1 = _layer_norm(DEEPNORM_ALPHA * x_ref[...] + hmix, g1_ref[...], b1_ref[...])
    x1_ref[...] = x1

    logits = jnp.dot(x1.astype(BF16), wr_ref[...], preferred_element_type=F32) + br_ref[...]
    lane = lax.broadcasted_iota(jnp.int32, (tm, LANES), 1)
    lane_f = lane.astype(F32)
    vals = jnp.where(lane < N_EXPERTS, logits, -jnp.inf)
    idx_tile = jnp.zeros((tm, LANES), F32)
    val_tile = jnp.zeros((tm, LANES), F32)
    top0 = None
    for kk in range(TOP_K):
        mk = jnp.max(vals, axis=-1, keepdims=True)
        ik = jnp.min(jnp.where(vals == mk, lane_f, float(LANES)), axis=-1, keepdims=True)
        vals = jnp.where(lane_f == ik, -jnp.inf, vals)
        if kk == 0:
            top0 = mk
        idx_tile = jnp.where(lane == kk, ik, idx_tile)
        val_tile = jnp.where(lane == kk, jnp.exp(mk - top0), val_tile)
    idx_ref[...] = idx_tile
    gate_ref[...] = val_tile / jnp.sum(val_tile, axis=-1, keepdims=True)


def _tail(proj, o1, o2, o3, l1, l2, l3, x2d, sw, sbias, slg, slb, wso, wao, wo, g1, b1, wr, br):
    S = x2d.shape[0]
    tm = TAIL_TM
    row = lambda w: pl.BlockSpec((tm, w), lambda i: (i, 0))

    def const(shape):
        nd = len(shape)
        return pl.BlockSpec(shape, lambda i: (0,) * nd, pipeline_mode=pl.Buffered(1))

    wide = lambda cb: pl.BlockSpec((tm, D_MODEL), lambda i: (i, cb))
    return pl.pallas_call(
        _tail_kernel,
        grid=(S // tm,),
        in_specs=[wide(0), wide(1), wide(2),
                  row(GROUP_W), row(GROUP_W), row(GROUP_W), row(LANES), row(LANES), row(LANES),
                  row(D_MODEL),
                  const(sw.shape), const(sbias.shape), const(slg.shape), const(slb.shape),
                  const(wso.shape), const(wao.shape), const(wo.shape),
                  const(g1.shape), const(b1.shape), const(wr.shape), const(br.shape)],
        out_specs=[row(D_MODEL), row(LANES), row(LANES)],
        out_shape=[jax.ShapeDtypeStruct((S, D_MODEL), F32),
                   jax.ShapeDtypeStruct((S, LANES), F32),
                   jax.ShapeDtypeStruct((S, LANES), F32)],
        compiler_params=_cparams(("arbitrary",)),
        name="tail",
    )(proj, proj, proj, o1, o2, o3, l1, l2, l3, x2d, sw, sbias, slg, slb, wso, wao, wo, g1, b1, wr, br)


def _row_gather(idx_ref, n, src_hbm, dst, sem):
    def body(r, carry):
        t = idx_ref[0, 0, r]
        pltpu.make_async_copy(src_hbm.at[pl.ds(t, 1)], dst.at[pl.ds(r, 1)], sem).start()
        return carry
    lax.fori_loop(0, n, body, 0, unroll=8)


def _row_gather_unrolled(idx_ref, n, src_hbm, dst, sem):
    for r in range(n):
        t = idx_ref[0, 0, r]
        pltpu.make_async_copy(src_hbm.at[pl.ds(t, 1)], dst.at[pl.ds(r, 1)], sem).start()


def _stream_expert_weights(i, t0_ref, t1_ref, te_ref, nt_ref, w_hbm, stg, wsem, wbf):
    rows = stg.shape[1]
    ntask = nt_ref[0]

    def start(k):
        c = k % W_CHUNKS
        src = w_hbm.at[te_ref[k], pl.ds(pl.multiple_of(c * rows, rows), rows), :]
        pltpu.make_async_copy(src, stg.at[k % 2], wsem.at[k % 2]).start()

    @pl.when(i == 0)
    def _():
        start(0)
        start(1)

    def body(k, carry):
        slot = k % 2
        pltpu.make_async_copy(w_hbm.at[0, pl.ds(0, rows), :], stg.at[slot], wsem.at[slot]).wait()
        par = (k // W_CHUNKS) % 2
        c = k % W_CHUNKS
        wbf[par, pl.ds(pl.multiple_of(c * rows, rows), rows), :] = stg[slot].astype(BF16)

        @pl.when(k + 2 < ntask)
        def _():
            start(k + 2)
        return carry

    lax.fori_loop(t0_ref[i], t1_ref[i], body, 0)


def _moe_up_kernel(be_ref, nu_ref, par_ref, t0_ref, t1_ref, te_ref, nt_ref,
                   tok_ref, tokn_ref, x_hbm, w_hbm, b_ref, h_ref, xbuf, xsem, wbf, stg, wsem):
    i = pl.program_id(0)
    nused = nu_ref[0]
    slot = i % 2

    @pl.when(i == 0)
    def _():
        _row_gather(tok_ref, MOE_TM, x_hbm, xbuf.at[0], xsem.at[0])

    _stream_expert_weights(i, t0_ref, t1_ref, te_ref, nt_ref, w_hbm, stg, wsem, wbf)

    def compute(gather_next):
        pltpu.make_async_copy(x_hbm.at[pl.ds(0, MOE_TM)], xbuf.at[slot], xsem.at[slot]).wait()
        xb = xbuf[slot].astype(BF16)
        w = wbf.at[par_ref[i]]
        nchunk = D_EXPERT // COL_BLK
        per = MOE_TM // nchunk
        for c in range(nchunk):
            if gather_next:
                for r in range(c * per, (c + 1) * per):
                    t = tokn_ref[0, 0, r]
                    pltpu.make_async_copy(x_hbm.at[pl.ds(t, 1)], xbuf.at[1 - slot, pl.ds(r, 1)],
                                          xsem.at[1 - slot]).start()
            cs = slice(c * COL_BLK, (c + 1) * COL_BLK)
            us = slice(D_EXPERT + c * COL_BLK, D_EXPERT + (c + 1) * COL_BLK)
            g = jnp.dot(xb, w[:, cs], preferred_element_type=F32) + b_ref[0, :, cs]
            up = jnp.dot(xb, w[:, us], preferred_element_type=F32) + b_ref[0, :, us]
            g = jnp.minimum(g, SWIGLU_LIMIT)
            up = jnp.clip(up, -SWIGLU_LIMIT, SWIGLU_LIMIT)
            act = (up + 1.0) * (g * (1.0 / (1.0 + jnp.exp(-SWIGLU_ALPHA * g))))
            h_ref[:, cs] = act.astype(BF16)

    @pl.when(i + 1 < nused)
    def _():
        compute(True)

    @pl.when(i + 1 == nused)
    def _():
        compute(False)

    @pl.when(i >= nused)
    def _():
        h_ref[...] = jnp.zeros(h_ref.shape, h_ref.dtype)


def _moe_up(sched, row_tok3, x1, wgu, bgu):
    nblk = sched[0].shape[0]
    rows = D_MODEL // W_CHUNKS
    smem_blk = lambda off: pl.BlockSpec(
        (1, 1, MOE_TM), lambda i, *_: (jnp.minimum(i + off, nblk - 1), 0, 0), memory_space=pltpu.SMEM)
    grid_spec = pltpu.PrefetchScalarGridSpec(
        num_scalar_prefetch=len(sched),
        grid=(nblk,),
        in_specs=[smem_blk(0), smem_blk(1),
                  pl.BlockSpec(memory_space=pl.ANY),
                  pl.BlockSpec(memory_space=pl.ANY),
                  pl.BlockSpec((1, 1, 2 * D_EXPERT), lambda i, be, *_: (be[i], 0, 0))],
        out_specs=pl.BlockSpec((MOE_TM, D_EXPERT), lambda i, *_: (i, 0)),
        scratch_shapes=[pltpu.VMEM((2, MOE_TM, D_MODEL), F32), pltpu.SemaphoreType.DMA((2,)),
                        pltpu.VMEM((2, D_MODEL, 2 * D_EXPERT), BF16),
                        pltpu.VMEM((2, rows, 2 * D_EXPERT), F32), pltpu.SemaphoreType.DMA((2,))],
    )
    return pl.pallas_call(
        _moe_up_kernel,
        grid_spec=grid_spec,
        out_shape=jax.ShapeDtypeStruct((nblk * MOE_TM, D_EXPERT), BF16),
        compiler_params=_cparams(("arbitrary",)),
        name="moe_up",
    )(*sched, row_tok3, row_tok3, x1, wgu, bgu)


def _moe_down_kernel(be_ref, nu_ref, par_ref, t0_ref, t1_ref, te_ref, nt_ref,
                     h_ref, w_hbm, b_ref, y_ref, wbf, stg, wsem):
    i = pl.program_id(0)
    _stream_expert_weights(i, t0_ref, t1_ref, te_ref, nt_ref, w_hbm, stg, wsem, wbf)

    @pl.when(i < nu_ref[0])
    def _():
        y_ref[...] = jnp.dot(h_ref[...], wbf[par_ref[i]], preferred_element_type=F32) + b_ref[0]

    @pl.when(i >= nu_ref[0])
    def _():
        y_ref[...] = jnp.zeros(y_ref.shape, y_ref.dtype)


def _moe_down(sched, h, wd, bd):
    nblk = sched[0].shape[0]
    rows = D_EXPERT // W_CHUNKS
    grid_spec = pltpu.PrefetchScalarGridSpec(
        num_scalar_prefetch=len(sched),
        grid=(nblk,),
        in_specs=[pl.BlockSpec((MOE_TM, D_EXPERT), lambda i, be, nu, *_: (jnp.minimum(i, nu[0] - 1), 0)),
                  pl.BlockSpec(memory_space=pl.ANY),
                  pl.BlockSpec((1, 1, D_MODEL), lambda i, be, *_: (be[i], 0, 0))],
        out_specs=pl.BlockSpec((MOE_TM, D_MODEL), lambda i, *_: (i, 0)),
        scratch_shapes=[pltpu.VMEM((2, D_EXPERT, D_MODEL), BF16),
                        pltpu.VMEM((2, rows, D_MODEL), F32), pltpu.SemaphoreType.DMA((2,))],
    )
    return pl.pallas_call(
        _moe_down_kernel,
        grid_spec=grid_spec,
        out_shape=jax.ShapeDtypeStruct((nblk * MOE_TM, D_MODEL), F32),
        compiler_params=_cparams(("arbitrary",)),
        name="moe_down",
    )(*sched, h, wd, bd)


def _combine_kernel(pos_ref, posn_ref, gate_ref, x1_ref, g2_ref, b2_ref, y_hbm, o_ref, ybuf, sem):
    i = pl.program_id(0)
    n = pl.num_programs(0)
    slot = i % 2
    nrow = TOP_K * COMB_TM

    @pl.when(i == 0)
    def _():
        _row_gather(pos_ref, nrow, y_hbm, ybuf.at[0], sem.at[0])

    @pl.when(i + 1 < n)
    def _():
        _row_gather_unrolled(posn_ref, nrow, y_hbm, ybuf.at[1 - slot], sem.at[1 - slot])

    pltpu.make_async_copy(y_hbm.at[pl.ds(0, nrow)], ybuf.at[slot], sem.at[slot]).wait()
    gate = gate_ref[...]
    y = jnp.zeros((COMB_TM, D_MODEL), F32)
    for kk in range(TOP_K):
        y = y + gate[:, kk:kk + 1] * ybuf[slot, kk * COMB_TM:(kk + 1) * COMB_TM, :]
    o_ref[...] = _layer_norm(DEEPNORM_ALPHA * x1_ref[...] + y, g2_ref[...], b2_ref[...])


def _combine(pos3, gate, x1, g2, b2, yb):
    S = x1.shape[0]
    nt = S // COMB_TM
    nrow = TOP_K * COMB_TM
    smem_blk = lambda off: pl.BlockSpec(
        (1, 1, nrow), lambda i: (jnp.minimum(i + off, nt - 1), 0, 0), memory_space=pltpu.SMEM)
    return pl.pallas_call(
        _combine_kernel,
        grid=(nt,),
        in_specs=[smem_blk(0), smem_blk(1),
                  pl.BlockSpec((COMB_TM, LANES), lambda i: (i, 0)),
                  pl.BlockSpec((COMB_TM, D_MODEL), lambda i: (i, 0)),
                  pl.BlockSpec((1, D_MODEL), lambda i: (0, 0)),
                  pl.BlockSpec((1, D_MODEL), lambda i: (0, 0)),
                  pl.BlockSpec(memory_space=pl.ANY)],
        out_specs=pl.BlockSpec((COMB_TM, D_MODEL), lambda i: (i, 0)),
        out_shape=jax.ShapeDtypeStruct((S, D_MODEL), F32),
        scratch_shapes=[pltpu.VMEM((2, nrow, D_MODEL), F32), pltpu.SemaphoreType.DMA((2,))],
        compiler_params=_cparams(("arbitrary",)),
        name="combine",
    )(pos3, pos3, gate, x1, g2, b2, yb)


def _rope_tables(pos):
    half = ROT_DIM // 2
    inv_freq = ROPE_THETA ** (-jnp.arange(0, ROT_DIM, 2, dtype=F32) / ROT_DIM)
    ang = pos.astype(F32)[:, None] * inv_freq
    cos, sin = jnp.cos(ang), jnp.sin(ang)
    S = pos.shape[0]
    one = jnp.ones((S, HEAD_DIM - ROT_DIM), F32)
    zero = jnp.zeros((S, HEAD_DIM - ROT_DIM), F32)
    z16 = jnp.zeros((S, half), F32)
    rc = jnp.concatenate([cos, cos, one], axis=1)
    ra = jnp.concatenate([-sin, z16, zero], axis=1)
    rb = jnp.concatenate([z16, sin, zero], axis=1)
    return rc, ra, rb


def _routing(idx, S):
    n = S * TOP_K
    nblk = n // MOE_TM + N_EXPERTS
    onehot = (idx[:, :, None] == jnp.arange(N_EXPERTS, dtype=jnp.int32)[None, None, :])
    member = jnp.sum(onehot, axis=1, dtype=jnp.int32)
    before = jnp.cumsum(member, axis=0) - member
    rank = jnp.take_along_axis(before, idx, axis=1)
    counts = jnp.sum(member, axis=0)
    padded = (counts + MOE_TM - 1) // MOE_TM * MOE_TM
    pends = jnp.cumsum(padded)
    pstarts = pends - padded
    dest = pstarts[idx] + rank
    tok = jnp.broadcast_to(jnp.arange(S, dtype=jnp.int32)[:, None], (S, TOP_K))
    row_tok = jnp.zeros((nblk * MOE_TM,), jnp.int32).at[dest.reshape(-1)].set(tok.reshape(-1))
    starts = jnp.arange(nblk, dtype=jnp.int32) * MOE_TM
    block_e = jnp.minimum(jnp.sum(pends[None, :] <= starts[:, None], axis=1, dtype=jnp.int32), N_EXPERTS - 1)
    nused = (pends[-1:] // MOE_TM).astype(jnp.int32)

    e_ids = jnp.arange(N_EXPERTS, dtype=jnp.int32)
    nb_e = (padded // MOE_TM).astype(jnp.int32)
    bstart_e = (pstarts // MOE_TM).astype(jnp.int32)
    nonempty = nb_e > 0
    run_of_e = jnp.cumsum(nonempty.astype(jnp.int32)) - 1
    nruns = jnp.sum(nonempty.astype(jnp.int32))
    run_e = jnp.sum(jnp.where(nonempty[None, :] & (run_of_e[None, :] == e_ids[:, None]), e_ids[None, :], 0), axis=1)
    task_e = jnp.repeat(run_e, W_CHUNKS).astype(jnp.int32)
    ntask = (W_CHUNKS * nruns).reshape(1).astype(jnp.int32)
    blk = jnp.arange(nblk, dtype=jnp.int32)
    used = blk < nused[0]
    r_i = run_of_e[block_e]
    j_i = blk - bstart_e[block_e]
    n_i = jnp.maximum(nb_e[block_e], 1)
    t0 = W_CHUNKS * (r_i + 1) + (W_CHUNKS * j_i) // n_i
    t1 = W_CHUNKS * (r_i + 1) + (W_CHUNKS * (j_i + 1)) // n_i
    t0 = jnp.where(blk == 0, 0, jnp.where(used, jnp.minimum(t0, ntask[0]), ntask[0])).astype(jnp.int32)
    t1 = jnp.where(used, jnp.minimum(t1, ntask[0]), ntask[0]).astype(jnp.int32)
    par = (r_i % 2).astype(jnp.int32)
    sched = (block_e, nused, par, t0, t1, task_e, ntask)
    return dest.astype(jnp.int32), row_tok, sched


def kernel(x, positions, w_in, w_attn_out, sgu_ln_g, sgu_ln_b, sgu_w, sgu_b, w_sgu_out, w_out,
           ln1_g, ln1_b, w_router, b_router, w_gate_up, b_gate_up, w_down, b_down, ln2_g, ln2_b):
    B, S, D = x.shape
    assert B == 1 and D == D_MODEL and w_in.shape[0] == DEPTH
    assert S % PROJ_TM == 0 and S % (DIL_PATTERNS[-1][1] * ATTN_QB * ATTN_BLK) == 0
    x2d = x.reshape(S, D)

    a0, a1, a2 = ATTN_WIDTH, 2 * ATTN_WIDTH, 3 * ATTN_WIDTH
    z1 = a2 + 2 * SGU_WIDTH
    wi = w_in[0]
    w_cat = jnp.concatenate([wi[:, a2:z1], wi[:, z1:z1 + D], wi[:, z1 + D:],
                             wi[:, :a0], wi[:, a0:a1], wi[:, a1:a2]], axis=1).astype(BF16)
    rc, ra, rb = _rope_tables(positions[0])

    proj = _inproj(x2d, w_cat, rc, ra, rb)

    outs, lses = [], []
    for gi, (window, d) in enumerate(DIL_PATTERNS):
        assert window // (2 * d) == ATTN_BLK // 2
        L = S // d
        if d == 1:
            o, l = _attention(proj.reshape(1, S, IN_WIDTH), Q_BLK0 + gi, K_BLK0 + gi, V_BLK0 + gi)
        else:
            cols = [proj[:, (b0 + gi) * GROUP_W:(b0 + gi + 1) * GROUP_W] for b0 in (Q_BLK0, K_BLK0, V_BLK0)]
            qkv = jnp.concatenate(cols, axis=1).reshape(L, d, 3 * GROUP_W).transpose(1, 0, 2)
            o, l = _attention(qkv, 0, 1, 2)
        outs.append(o.transpose(1, 0, 2).reshape(S, GROUP_W))
        lses.append(l.transpose(1, 0, 2).reshape(S, LANES))

    sbias = jnp.repeat(sgu_b[0].T, CHUNK, axis=1)
    wr = jnp.pad(w_router[0], ((0, 0), (0, LANES - N_EXPERTS))).astype(BF16)
    br = jnp.pad(b_router[0], (0, LANES - N_EXPERTS)).reshape(1, LANES)
    x1, idx_f, gate = _tail(
        proj, outs[0], outs[1], outs[2], lses[0], lses[1], lses[2], x2d,
        sgu_w[0].astype(BF16), sbias, sgu_ln_g[0].reshape(1, -1), sgu_ln_b[0].reshape(1, -1),
        w_sgu_out[0].astype(BF16), w_attn_out[0].astype(BF16), w_out[0].astype(BF16),
        ln1_g[0].reshape(1, -1), ln1_b[0].reshape(1, -1), wr, br)

    idx = idx_f[:, :TOP_K].astype(jnp.int32)
    dest, row_tok, sched = _routing(idx, S)
    nblk = sched[0].shape[0]

    h = _moe_up(sched, row_tok.reshape(nblk, 1, MOE_TM), x1, w_gate_up[0], b_gate_up[0].reshape(N_EXPERTS, 1, -1))
    yb = _moe_down(sched, h, w_down[0], b_down[0].reshape(N_EXPERTS, 1, -1))

    nt = S // COMB_TM
    pos3 = dest.reshape(nt, COMB_TM, TOP_K).transpose(0, 2, 1).reshape(nt, 1, TOP_K * COMB_TM)
    out = _combine(pos3, gate, x1, ln2_g[0].reshape(1, -1), ln2_b[0].reshape(1, -1), yb)
    return out.reshape(B, S, D)
```

```python
import numpy as np
import jax
import jax.numpy as jnp
from jax import lax
from jax.experimental import pallas as pl
from jax.experimental.pallas import tpu as pltpu

F32 = jnp.float32
BF16 = jnp.bfloat16

D_MODEL = 2048
HEAD_DIM = 128
ROT_DIM = HEAD_DIM // 4
ROPE_THETA = 500000.0
DIL_PATTERNS = ((128, 1), (512, 4), (2048, 16))
N_GROUPS = len(DIL_PATTERNS)
HEADS = 4
GROUP_W = HEADS * HEAD_DIM
ATTN_WIDTH = N_GROUPS * GROUP_W
CHUNK = 128
SGU_GROUPS = 8
SGU_WIDTH = SGU_GROUPS * CHUNK
IN_WIDTH = 3 * ATTN_WIDTH + 2 * SGU_WIDTH + 2 * D_MODEL
N_EXPERTS = 32
TOP_K = 4
D_EXPERT = D_MODEL
SWIGLU_LIMIT = 7.0
SWIGLU_ALPHA = 1.702
LN_EPS = 1e-5
DEPTH = 1
DEEPNORM_ALPHA = (2 * DEPTH) ** 0.25
NEG_INF = -1e30

LANES = 128
V7X_VMEM_BYTES = 64 * 1024 * 1024
VMEM_LIMIT = V7X_VMEM_BYTES - 8 * 1024 * 1024

COL_BLK = 512
PROJ_TM = 512
PROJ_CHUNKS = 3
ATTN_BLK = 128
ATTN_QB = 4
ATTN_AHEAD = 10
TAIL_TM = 256
MOE_TM = 256
W_CHUNKS = 8
X_SLOTS = 3
COMB_TM = 128

Z_BLK0, GA_BLK0, GB_BLK0, Q_BLK0, K_BLK0, V_BLK0 = 0, 4, 8, 12, 15, 18
N_PROJ_BLKS = IN_WIDTH // COL_BLK
assert N_PROJ_BLKS == 21 and N_PROJ_BLKS % PROJ_CHUNKS == 0


def _cparams(sem):
    return pltpu.CompilerParams(dimension_semantics=sem, vmem_limit_bytes=VMEM_LIMIT)


def _layer_norm(v, g, b):
    mu = jnp.mean(v, axis=-1, keepdims=True)
    c = v - mu
    var = jnp.mean(c * c, axis=-1, keepdims=True)
    return c * lax.rsqrt(var + LN_EPS) * g + b


def _epilogue_kind(blk):
    if blk < GA_BLK0:
        return "gelu"
    if blk < Q_BLK0:
        return "sigmoid"
    if blk < V_BLK0:
        return "rope"
    return "plain"


def _inproj_kernel(x_ref, w_ref, rc_ref, ra_ref, rb_ref, o_ref, xb_ref):
    j = pl.program_id(1)

    @pl.when(j == 0)
    def _():
        xb_ref[...] = x_ref[...].astype(BF16)

    def epilogue(acc, kind):
        if kind == "gelu":
            return 0.5 * acc * (1.0 + lax.erf(acc * (2.0 ** -0.5)))
        if kind == "sigmoid":
            return 1.0 / (1.0 + jnp.exp(-acc))
        if kind == "rope":
            c = jnp.concatenate([rc_ref[...]] * HEADS, axis=1)
            a = jnp.concatenate([ra_ref[...]] * HEADS, axis=1)
            b = jnp.concatenate([rb_ref[...]] * HEADS, axis=1)
            half = ROT_DIM // 2
            up = pltpu.roll(acc, COL_BLK - half, axis=1)
            dn = pltpu.roll(acc, half, axis=1)
            return acc * c + up * a + dn * b
        return acc

    patterns = {}
    for t in range(N_PROJ_BLKS // PROJ_CHUNKS):
        kinds = tuple(_epilogue_kind(t * PROJ_CHUNKS + c) for c in range(PROJ_CHUNKS))
        patterns.setdefault(kinds, []).append(t)
    for kinds, tiles in patterns.items():
        cond = j == tiles[0]
        for t in tiles[1:]:
            cond = cond | (j == t)

        @pl.when(cond)
        def _(kinds=kinds):
            for c, kind in enumerate(kinds):
                cs = slice(c * COL_BLK, (c + 1) * COL_BLK)
                acc = jnp.dot(xb_ref[...], w_ref[:, cs], preferred_element_type=F32)
                o_ref[:, cs] = epilogue(acc, kind).astype(BF16)


def _inproj(x2d, w_cat, rc, ra, rb):
    S = x2d.shape[0]
    tn = PROJ_CHUNKS * COL_BLK
    return pl.pallas_call(
        _inproj_kernel,
        grid=(S // PROJ_TM, IN_WIDTH // tn),
        in_specs=[
            pl.BlockSpec((PROJ_TM, D_MODEL), lambda i, j: (i, 0)),
            pl.BlockSpec((D_MODEL, tn), lambda i, j: (0, j)),
            pl.BlockSpec((PROJ_TM, LANES), lambda i, j: (i, 0)),
            pl.BlockSpec((PROJ_TM, LANES), lambda i, j: (i, 0)),
            pl.BlockSpec((PROJ_TM, LANES), lambda i, j: (i, 0)),
        ],
        out_specs=pl.BlockSpec((PROJ_TM, tn), lambda i, j: (i, j)),
        out_shape=jax.ShapeDtypeStruct((S, IN_WIDTH), BF16),
        scratch_shapes=[pltpu.VMEM((PROJ_TM, D_MODEL), BF16)],
        compiler_params=_cparams(("arbitrary", "arbitrary")),
        name="inproj",
    )(x2d, w_cat, rc, ra, rb)


def _attn_kernel(q_ref, kp_ref, kc_ref, kn_ref, vp_ref, vc_ref, vn_ref, band_ref, o_ref, l_ref):
    i = pl.program_id(1)
    n = pl.num_programs(1)
    k = jnp.concatenate([kp_ref[0], kc_ref[0], kn_ref[0]], axis=0)
    v = jnp.concatenate([vp_ref[0], vc_ref[0], vn_ref[0]], axis=0)
    hq = ATTN_BLK // 2
    col = lax.broadcasted_iota(jnp.int32, (hq, 2 * ATTN_BLK), 1)
    lane = lax.broadcasted_iota(jnp.int32, (hq, LANES), 1)
    scale = HEAD_DIM ** -0.5
    first_bias = jnp.where(col >= jnp.where(i > 0, 0, ATTN_BLK), band_ref[0], NEG_INF)
    last_bias = jnp.where(col < jnp.where(i < n - 1, 2 * ATTN_BLK, ATTN_BLK), band_ref[1], NEG_INF)

    units = [(s, hf, h) for s in range(ATTN_QB) for hf in range(2) for h in range(HEADS)]

    def scores(s, hf, h):
        qs = slice(s * ATTN_BLK + hf * hq, s * ATTN_BLK + (hf + 1) * hq)
        ks = slice((s + hf) * ATTN_BLK, (s + hf + 2) * ATTN_BLK)
        hs = slice(h * HEAD_DIM, (h + 1) * HEAD_DIM)
        return lax.dot_general(q_ref[0, qs, hs], k[ks, hs], (((1,), (1,)), ((), ())),
                               preferred_element_type=F32)

    pending = [scores(*u) for u in units[:ATTN_AHEAD]]
    lse_tile = None
    for j, (s, hf, h) in enumerate(units):
        raw = pending.pop(0)
        if j + ATTN_AHEAD < len(units):
            pending.append(scores(*units[j + ATTN_AHEAD]))
        if s == 0 and hf == 0:
            bias = first_bias
        elif s == ATTN_QB - 1 and hf == 1:
            bias = last_bias
        else:
            bias = band_ref[hf]
        qs = slice(s * ATTN_BLK + hf * hq, s * ATTN_BLK + (hf + 1) * hq)
        ks = slice((s + hf) * ATTN_BLK, (s + hf + 2) * ATTN_BLK)
        hs = slice(h * HEAD_DIM, (h + 1) * HEAD_DIM)
        sc = raw * scale + bias
        m = jnp.max(sc, axis=-1, keepdims=True)
        p = jnp.exp(sc - m)
        den = jnp.sum(p, axis=-1, keepdims=True)
        o = jnp.dot(p.astype(BF16), v[ks, hs], preferred_element_type=F32) / den
        o_ref[0, qs, hs] = o.astype(BF16)
        if h == 0:
            lse_tile = jnp.zeros((hq, LANES), F32)
        lse_tile = jnp.where(lane == h, m + jnp.log(den), lse_tile)
        if h == HEADS - 1:
            l_ref[0, qs, :] = lse_tile


def _band_bias():
    hq = ATTN_BLK // 2
    out = np.empty((2, hq, 2 * ATTN_BLK), np.float32)
    for hf in range(2):
        i = np.arange(hq)[:, None] + hf * hq
        c = np.arange(2 * ATTN_BLK)[None, :] + hf * ATTN_BLK
        rel = c - ATTN_BLK - i
        out[hf] = np.where(np.abs(rel) <= hq, 0.0, NEG_INF)
    return jnp.asarray(out)


def _attention(arr, qb, kb, vb):
    d, L, _ = arr.shape
    assert ATTN_QB >= 2 and L % (ATTN_QB * ATTN_BLK) == 0
    nb = L // ATTN_BLK
    big = (1, ATTN_QB * ATTN_BLK, GROUP_W)
    small = (1, ATTN_BLK, GROUP_W)
    cur = lambda cb: pl.BlockSpec(big, lambda r, i: (r, i, cb))
    prev = lambda cb: pl.BlockSpec(small, lambda r, i: (r, jnp.maximum(ATTN_QB * i - 1, 0), cb))
    nxt = lambda cb: pl.BlockSpec(small, lambda r, i: (r, jnp.minimum(ATTN_QB * (i + 1), nb - 1), cb))
    return pl.pallas_call(
        _attn_kernel,
        grid=(d, nb // ATTN_QB),
        in_specs=[cur(qb), prev(kb), cur(kb), nxt(kb), prev(vb), cur(vb), nxt(vb),
                  pl.BlockSpec((2, ATTN_BLK // 2, 2 * ATTN_BLK), lambda r, i: (0, 0, 0))],
        out_specs=[pl.BlockSpec(big, lambda r, i: (r, i, 0)),
                   pl.BlockSpec((1, ATTN_QB * ATTN_BLK, LANES), lambda r, i: (r, i, 0))],
        out_shape=[jax.ShapeDtypeStruct((d, L, GROUP_W), BF16),
                   jax.ShapeDtypeStruct((d, L, LANES), F32)],
        compiler_params=_cparams(("arbitrary", "arbitrary")),
        name=f"attn_d{d}",
    )(arr, arr, arr, arr, arr, arr, arr, _band_bias())


def _tail_kernel(z_ref, ga_ref, gb_ref, o1_ref, o2_ref, o3_ref, l1_ref, l2_ref, l3_ref, x_ref,
                 sw_ref, sbias_ref, slg_ref, slb_ref, wso_ref, wao_ref, wo_ref,
                 g1_ref, b1_ref, wr_ref, br_ref,
                 x1_ref, idx_ref, gate_ref):
    tm = z_ref.shape[0]
    z = z_ref[...]
    u = z[:, :SGU_WIDTH].astype(F32)
    vn = _layer_norm(z[:, SGU_WIDTH:].astype(F32), slg_ref[...], slb_ref[...]).astype(BF16)
    rows = []
    for c in range(tm // CHUNK):
        cols = []
        for g in range(SGU_GROUPS):
            blk = vn[c * CHUNK:(c + 1) * CHUNK, g * CHUNK:(g + 1) * CHUNK]
            cols.append(jnp.dot(sw_ref[g], blk, preferred_element_type=F32))
        rows.append(jnp.concatenate(cols, axis=1) + sbias_ref[...])
    mixed = jnp.concatenate(rows, axis=0)
    y_b = jnp.dot((u * mixed).astype(BF16), wso_ref[...], preferred_element_type=F32)

    l1, l2, l3 = l1_ref[...], l2_ref[...], l3_ref[...]
    m = jnp.maximum(jnp.maximum(l1, l2), l3)
    e1, e2, e3 = jnp.exp(l1 - m), jnp.exp(l2 - m), jnp.exp(l3 - m)
    inv = 1.0 / (e1 + e2 + e3)
    w1, w2, w3 = e1 * inv, e2 * inv, e3 * inv
    parts = []
    for h in range(HEADS):
        hs = slice(h * HEAD_DIM, (h + 1) * HEAD_DIM)
        parts.append(w1[:, h:h + 1] * o1_ref[:, hs].astype(F32)
                     + w2[:, h:h + 1] * o2_ref[:, hs].astype(F32)
                     + w3[:, h:h + 1] * o3_ref[:, hs].astype(F32))
    o_a = jnp.concatenate(parts, axis=1).astype(BF16)
    y_a = jnp.dot(o_a, wao_ref[...], preferred_element_type=F32)

    merged = ga_ref[...].astype(F32) * y_a + gb_ref[...].astype(F32) * y_b
    hmix = jnp.dot(merged.astype(BF16), wo_ref[...], preferred_element_type=F32)
    x1 = _layer_norm(DEEPNORM_ALPHA * x_ref[...] + hmix, g1_ref[...], b1_ref[...])
    x1_ref[...] = x1

    logits = jnp.dot(x1.astype(BF16), wr_ref[...], preferred_element_type=F32) + br_ref[...]
    lane = lax.broadcasted_iota(jnp.int32, (tm, LANES), 1)
    lane_f = lane.astype(F32)
    vals = jnp.where(lane < N_EXPERTS, logits, -jnp.inf)
    idx_tile = jnp.zeros((tm, LANES), F32)
    val_tile = jnp.zeros((tm, LANES), F32)
    top0 = None
    for kk in range(TOP_K):
        mk = jnp.max(vals, axis=-1, keepdims=True)
        ik = jnp.min(jnp.where(vals == mk, lane_f, float(LANES)), axis=-1, keepdims=True)
        vals = jnp.where(lane_f == ik, -jnp.inf, vals)
        if kk == 0:
            top0 = mk
        idx_tile = jnp.where(lane == kk, ik, idx_tile)
        val_tile = jnp.where(lane == kk, jnp.exp(mk - top0), val_tile)
    idx_ref[...] = idx_tile
    gate_ref[...] = val_tile / jnp.sum(val_tile, axis=-1, keepdims=True)


def _tail(proj, o1, o2, o3, l1, l2, l3, x2d, sw, sbias, slg, slb, wso, wao, wo, g1, b1, wr, br):
    S = x2d.shape[0]
    tm = TAIL_TM
    row = lambda w: pl.BlockSpec((tm, w), lambda i: (i, 0))

    def const(shape):
        nd = len(shape)
        return pl.BlockSpec(shape, lambda i: (0,) * nd, pipeline_mode=pl.Buffered(1))

    wide = lambda cb: pl.BlockSpec((tm, D_MODEL), lambda i: (i, cb))
    return pl.pallas_call(
        _tail_kernel,
        grid=(S // tm,),
        in_specs=[wide(0), wide(1), wide(2),
                  row(GROUP_W), row(GROUP_W), row(GROUP_W), row(LANES), row(LANES), row(LANES),
                  row(D_MODEL),
                  const(sw.shape), const(sbias.shape), const(slg.shape), const(slb.shape),
                  const(wso.shape), const(wao.shape), const(wo.shape),
                  const(g1.shape), const(b1.shape), const(wr.shape), const(br.shape)],
        out_specs=[row(D_MODEL), row(LANES), row(LANES)],
        out_shape=[jax.ShapeDtypeStruct((S, D_MODEL), F32),
                   jax.ShapeDtypeStruct((S, LANES), F32),
                   jax.ShapeDtypeStruct((S, LANES), F32)],
        compiler_params=_cparams(("arbitrary",)),
        name="tail",
    )(proj, proj, proj, o1, o2, o3, l1, l2, l3, x2d, sw, sbias, slg, slb, wso, wao, wo, g1, b1, wr, br)


def _row_gather(idx_ref, n, src_hbm, dst, sem):
    def body(r, carry):
        t = idx_ref[0, 0, r]
        pltpu.make_async_copy(src_hbm.at[pl.ds(t, 1)], dst.at[pl.ds(r, 1)], sem).start()
        return carry
    lax.fori_loop(0, n, body, 0, unroll=8)


def _row_gather_unrolled(idx_ref, n, src_hbm, dst, sem):
    for r in range(n):
        t = idx_ref[0, 0, r]
        pltpu.make_async_copy(src_hbm.at[pl.ds(t, 1)], dst.at[pl.ds(r, 1)], sem).start()


def _stream_expert_weights(i, t0_ref, t1_ref, te_ref, nt_ref, w_hbm, stg, wsem, wbf):
    rows = stg.shape[1]
    ntask = nt_ref[0]

    def start(k):
        c = k % W_CHUNKS
        src = w_hbm.at[te_ref[k], pl.ds(pl.multiple_of(c * rows, rows), rows), :]
        pltpu.make_async_copy(src, stg.at[k % 2], wsem.at[k % 2]).start(priority=1)

    @pl.when(i == 0)
    def _():
        start(0)
        start(1)

    def body(k, carry):
        slot = k % 2
        pltpu.make_async_copy(w_hbm.at[0, pl.ds(0, rows), :], stg.at[slot], wsem.at[slot]).wait()
        par = (k // W_CHUNKS) % 2
        c = k % W_CHUNKS
        wbf[par, pl.ds(pl.multiple_of(c * rows, rows), rows), :] = stg[slot].astype(BF16)

        @pl.when(k + 2 < ntask)
        def _():
            start(k + 2)
        return carry

    lax.fori_loop(t0_ref[i], t1_ref[i], body, 0)


def _moe_up_kernel(be_ref, nu_ref, par_ref, t0_ref, t1_ref, te_ref, nt_ref,
                   tok_ref, tok1_ref, tok2_ref, x_hbm, w_hbm, b_ref, h_ref, xbuf, xsem, wbf, stg, wsem):
    i = pl.program_id(0)
    nused = nu_ref[0]
    slot = i % X_SLOTS

    @pl.when(i == 0)
    def _():
        _row_gather(tok_ref, MOE_TM, x_hbm, xbuf.at[0], xsem.at[0])

    @pl.when((i == 0) & (nused > 1))
    def _():
        _row_gather(tok1_ref, MOE_TM, x_hbm, xbuf.at[1], xsem.at[1])

    _stream_expert_weights(i, t0_ref, t1_ref, te_ref, nt_ref, w_hbm, stg, wsem, wbf)

    def compute(gather_ahead):
        if gather_ahead:
            ahead = (i + 2) % X_SLOTS
            _row_gather_unrolled(tok2_ref, MOE_TM, x_hbm, xbuf.at[ahead], xsem.at[ahead])
        pltpu.make_async_copy(x_hbm.at[pl.ds(0, MOE_TM)], xbuf.at[slot], xsem.at[slot]).wait()
        xb = xbuf[slot].astype(BF16)
        w = wbf.at[par_ref[i]]
        for c in range(D_EXPERT // COL_BLK):
            cs = slice(c * COL_BLK, (c + 1) * COL_BLK)
            us = slice(D_EXPERT + c * COL_BLK, D_EXPERT + (c + 1) * COL_BLK)
            g = jnp.dot(xb, w[:, cs], preferred_element_type=F32) + b_ref[0, :, cs]
            up = jnp.dot(xb, w[:, us], preferred_element_type=F32) + b_ref[0, :, us]
            g = jnp.minimum(g, SWIGLU_LIMIT)
            up = jnp.clip(up, -SWIGLU_LIMIT, SWIGLU_LIMIT)
            act = (up + 1.0) * (g * (1.0 / (1.0 + jnp.exp(-SWIGLU_ALPHA * g))))
            h_ref[:, cs] = act.astype(BF16)

    @pl.when(i + 2 < nused)
    def _():
        compute(True)

    @pl.when((i + 2 >= nused) & (i < nused))
    def _():
        compute(False)

    @pl.when(i >= nused)
    def _():
        h_ref[...] = jnp.zeros(h_ref.shape, h_ref.dtype)


def _moe_up(sched, row_tok3, x1, wgu, bgu):
    nblk = sched[0].shape[0]
    rows = D_MODEL // W_CHUNKS
    smem_blk = lambda off: pl.BlockSpec(
        (1, 1, MOE_TM), lambda i, *_: (jnp.minimum(i + off, nblk - 1), 0, 0), memory_space=pltpu.SMEM)
    grid_spec = pltpu.PrefetchScalarGridSpec(
        num_scalar_prefetch=len(sched),
        grid=(nblk,),
        in_specs=[smem_blk(0), smem_blk(1), smem_blk(2),
                  pl.BlockSpec(memory_space=pl.ANY),
                  pl.BlockSpec(memory_space=pl.ANY),
                  pl.BlockSpec((1, 1, 2 * D_EXPERT), lambda i, be, *_: (be[i], 0, 0))],
        out_specs=pl.BlockSpec((MOE_TM, D_EXPERT), lambda i, *_: (i, 0)),
        scratch_shapes=[pltpu.VMEM((X_SLOTS, MOE_TM, D_MODEL), F32), pltpu.SemaphoreType.DMA((X_SLOTS,)),
                        pltpu.VMEM((2, D_MODEL, 2 * D_EXPERT), BF16),
                        pltpu.VMEM((2, rows, 2 * D_EXPERT), F32), pltpu.SemaphoreType.DMA((2,))],
    )
    return pl.pallas_call(
        _moe_up_kernel,
        grid_spec=grid_spec,
        out_shape=jax.ShapeDtypeStruct((nblk * MOE_TM, D_EXPERT), BF16),
        compiler_params=_cparams(("arbitrary",)),
        name="moe_up",
    )(*sched, row_tok3, row_tok3, row_tok3, x1, wgu, bgu)


def _moe_down_kernel(be_ref, nu_ref, par_ref, t0_ref, t1_ref, te_ref, nt_ref,
                     h_ref, w_hbm, b_ref, y_ref, wbf, stg, wsem):
    i = pl.program_id(0)
    _stream_expert_weights(i, t0_ref, t1_ref, te_ref, nt_ref, w_hbm, stg, wsem, wbf)

    @pl.when(i < nu_ref[0])
    def _():
        y_ref[...] = jnp.dot(h_ref[...], wbf[par_ref[i]], preferred_element_type=F32) + b_ref[0]

    @pl.when(i >= nu_ref[0])
    def _():
        y_ref[...] = jnp.zeros(y_ref.shape, y_ref.dtype)


def _moe_down(sched, h, wd, bd):
    nblk = sched[0].shape[0]
    rows = D_EXPERT // W_CHUNKS
    grid_spec = pltpu.PrefetchScalarGridSpec(
        num_scalar_prefetch=len(sched),
        grid=(nblk,),
        in_specs=[pl.BlockSpec((MOE_TM, D_EXPERT), lambda i, be, nu, *_: (jnp.minimum(i, nu[0] - 1), 0)),
                  pl.BlockSpec(memory_space=pl.ANY),
                  pl.BlockSpec((1, 1, D_MODEL), lambda i, be, *_: (be[i], 0, 0))],
        out_specs=pl.BlockSpec((MOE_TM, D_MODEL), lambda i, *_: (i, 0)),
        scratch_shapes=[pltpu.VMEM((2, D_EXPERT, D_MODEL), BF16),
                        pltpu.VMEM((2, rows, D_MODEL), F32), pltpu.SemaphoreType.DMA((2,))],
    )
    return pl.pallas_call(
        _moe_down_kernel,
        grid_spec=grid_spec,
        out_shape=jax.ShapeDtypeStruct((nblk * MOE_TM, D_MODEL), F32),
        compiler_params=_cparams(("arbitrary",)),
        name="moe_down",
    )(*sched, h, wd, bd)


def _combine_kernel(pos_ref, posn_ref, gate_ref, x1_ref, g2_ref, b2_ref, y_hbm, o_ref, ybuf, sem):
    i = pl.program_id(0)
    n = pl.num_programs(0)
    slot = i % 2
    nrow = TOP_K * COMB_TM

    @pl.when(i == 0)
    def _():
        _row_gather(pos_ref, nrow, y_hbm, ybuf.at[0], sem.at[0])

    @pl.when(i + 1 < n)
    def _():
        _row_gather_unrolled(posn_ref, nrow, y_hbm, ybuf.at[1 - slot], sem.at[1 - slot])

    pltpu.make_async_copy(y_hbm.at[pl.ds(0, nrow)], ybuf.at[slot], sem.at[slot]).wait()
    gate = gate_ref[...]
    y = jnp.zeros((COMB_TM, D_MODEL), F32)
    for kk in range(TOP_K):
        y = y + gate[:, kk:kk + 1] * ybuf[slot, kk * COMB_TM:(kk + 1) * COMB_TM, :]
    o_ref[...] = _layer_norm(DEEPNORM_ALPHA * x1_ref[...] + y, g2_ref[...], b2_ref[...])


def _combine(pos3, gate, x1, g2, b2, yb):
    S = x1.shape[0]
    nt = S // COMB_TM
    nrow = TOP_K * COMB_TM
    smem_blk = lambda off: pl.BlockSpec(
        (1, 1, nrow), lambda i: (jnp.minimum(i + off, nt - 1), 0, 0), memory_space=pltpu.SMEM)
    return pl.pallas_call(
        _combine_kernel,
        grid=(nt,),
        in_specs=[smem_blk(0), smem_blk(1),
                  pl.BlockSpec((COMB_TM, LANES), lambda i: (i, 0)),
                  pl.BlockSpec((COMB_TM, D_MODEL), lambda i: (i, 0)),
                  pl.BlockSpec((1, D_MODEL), lambda i: (0, 0)),
                  pl.BlockSpec((1, D_MODEL), lambda i: (0, 0)),
                  pl.BlockSpec(memory_space=pl.ANY)],
        out_specs=pl.BlockSpec((COMB_TM, D_MODEL), lambda i: (i, 0)),
        out_shape=jax.ShapeDtypeStruct((S, D_MODEL), F32),
        scratch_shapes=[pltpu.VMEM((2, nrow, D_MODEL), F32), pltpu.SemaphoreType.DMA((2,))],
        compiler_params=_cparams(("arbitrary",)),
        name="combine",
    )(pos3, pos3, gate, x1, g2, b2, yb)


def _rope_tables(pos):
    half = ROT_DIM // 2
    inv_freq = ROPE_THETA ** (-jnp.arange(0, ROT_DIM, 2, dtype=F32) / ROT_DIM)
    ang = pos.astype(F32)[:, None] * inv_freq
    cos, sin = jnp.cos(ang), jnp.sin(ang)
    S = pos.shape[0]
    one = jnp.ones((S, HEAD_DIM - ROT_DIM), F32)
    zero = jnp.zeros((S, HEAD_DIM - ROT_DIM), F32)
    z16 = jnp.zeros((S, half), F32)
    rc = jnp.concatenate([cos, cos, one], axis=1)
    ra = jnp.concatenate([-sin, z16, zero], axis=1)
    rb = jnp.concatenate([z16, sin, zero], axis=1)
    return rc, ra, rb


def _routing(idx, S):
    n = S * TOP_K
    nblk = n // MOE_TM + N_EXPERTS
    onehot = (idx[:, :, None] == jnp.arange(N_EXPERTS, dtype=jnp.int32)[None, None, :])
    member = jnp.sum(onehot, axis=1, dtype=jnp.int32)
    before = jnp.cumsum(member, axis=0) - member
    rank = jnp.take_along_axis(before, idx, axis=1)
    counts = jnp.sum(member, axis=0)
    padded = (counts + MOE_TM - 1) // MOE_TM * MOE_TM
    pends = jnp.cumsum(padded)
    pstarts = pends - padded
    dest = pstarts[idx] + rank
    tok = jnp.broadcast_to(jnp.arange(S, dtype=jnp.int32)[:, None], (S, TOP_K))
    row_tok = jnp.zeros((nblk * MOE_TM,), jnp.int32).at[dest.reshape(-1)].set(tok.reshape(-1))
    starts = jnp.arange(nblk, dtype=jnp.int32) * MOE_TM
    block_e = jnp.minimum(jnp.sum(pends[None, :] <= starts[:, None], axis=1, dtype=jnp.int32), N_EXPERTS - 1)
    nused = (pends[-1:] // MOE_TM).astype(jnp.int32)

    e_ids = jnp.arange(N_EXPERTS, dtype=jnp.int32)
    nb_e = (padded // MOE_TM).astype(jnp.int32)
    bstart_e = (pstarts // MOE_TM).astype(jnp.int32)
    nonempty = nb_e > 0
    run_of_e = jnp.cumsum(nonempty.astype(jnp.int32)) - 1
    nruns = jnp.sum(nonempty.astype(jnp.int32))
    run_e = jnp.sum(jnp.where(nonempty[None, :] & (run_of_e[None, :] == e_ids[:, None]), e_ids[None, :], 0), axis=1)
    task_e = jnp.repeat(run_e, W_CHUNKS).astype(jnp.int32)
    ntask = (W_CHUNKS * nruns).reshape(1).astype(jnp.int32)
    blk = jnp.arange(nblk, dtype=jnp.int32)
    used = blk < nused[0]
    r_i = run_of_e[block_e]
    j_i = blk - bstart_e[block_e]
    n_i = jnp.maximum(nb_e[block_e], 1)
    t0 = W_CHUNKS * (r_i + 1) + (W_CHUNKS * j_i) // n_i
    t1 = W_CHUNKS * (r_i + 1) + (W_CHUNKS * (j_i + 1)) // n_i
    t0 = jnp.where(blk == 0, 0, jnp.where(used, jnp.minimum(t0, ntask[0]), ntask[0])).astype(jnp.int32)
    t1 = jnp.where(used, jnp.minimum(t1, ntask[0]), ntask[0]).astype(jnp.int32)
    par = (r_i % 2).astype(jnp.int32)
    sched = (block_e, nused, par, t0, t1, task_e, ntask)
    return dest.astype(jnp.int32), row_tok, sched


def kernel(x, positions, w_in, w_attn_out, sgu_ln_g, sgu_ln_b, sgu_w, sgu_b, w_sgu_out, w_out,
           ln1_g, ln1_b, w_router, b_router, w_gate_up, b_gate_up, w_down, b_down, ln2_g, ln2_b):
    B, S, D = x.shape
    assert B == 1 and D == D_MODEL and w_in.shape[0] == DEPTH
    assert S % PROJ_TM == 0 and S % (DIL_PATTERNS[-1][1] * ATTN_QB * ATTN_BLK) == 0
    x2d = x.reshape(S, D)

    a0, a1, a2 = ATTN_WIDTH, 2 * ATTN_WIDTH, 3 * ATTN_WIDTH
    z1 = a2 + 2 * SGU_WIDTH
    wi = w_in[0]
    w_cat = jnp.concatenate([wi[:, a2:z1], wi[:, z1:z1 + D], wi[:, z1 + D:],
                             wi[:, :a0], wi[:, a0:a1], wi[:, a1:a2]], axis=1).astype(BF16)
    rc, ra, rb = _rope_tables(positions[0])

    proj = _inproj(x2d, w_cat, rc, ra, rb)

    outs, lses = [], []
    for gi, (window, d) in enumerate(DIL_PATTERNS):
        assert window // (2 * d) == ATTN_BLK // 2
        L = S // d
        if d == 1:
            o, l = _attention(proj.reshape(1, S, IN_WIDTH), Q_BLK0 + gi, K_BLK0 + gi, V_BLK0 + gi)
        else:
            cols = [proj[:, (b0 + gi) * GROUP_W:(b0 + gi + 1) * GROUP_W] for b0 in (Q_BLK0, K_BLK0, V_BLK0)]
            qkv = jnp.concatenate(cols, axis=1).reshape(L, d, 3 * GROUP_W).transpose(1, 0, 2)
            o, l = _attention(qkv, 0, 1, 2)
        outs.append(o.transpose(1, 0, 2).reshape(S, GROUP_W))
        lses.append(l.transpose(1, 0, 2).reshape(S, LANES))

    sbias = jnp.repeat(sgu_b[0].T, CHUNK, axis=1)
    wr = jnp.pad(w_router[0], ((0, 0), (0, LANES - N_EXPERTS))).astype(BF16)
    br = jnp.pad(b_router[0], (0, LANES - N_EXPERTS)).reshape(1, LANES)
    x1, idx_f, gate = _tail(
        proj, outs[0], outs[1], outs[2], lses[0], lses[1], lses[2], x2d,
        sgu_w[0].astype(BF16), sbias, sgu_ln_g[0].reshape(1, -1), sgu_ln_b[0].reshape(1, -1),
        w_sgu_out[0].astype(BF16), w_attn_out[0].astype(BF16), w_out[0].astype(BF16),
        ln1_g[0].reshape(1, -1), ln1_b[0].reshape(1, -1), wr, br)

    idx = idx_f[:, :TOP_K].astype(jnp.int32)
    dest, row_tok, sched = _routing(idx, S)
    nblk = sched[0].shape[0]

    h = _moe_up(sched, row_tok.reshape(nblk, 1, MOE_TM), x1, w_gate_up[0], b_gate_up[0].reshape(N_EXPERTS, 1, -1))
    yb = _moe_down(sched, h, w_down[0], b_down[0].reshape(N_EXPERTS, 1, -1))

    nt = S // COMB_TM
    pos3 = dest.reshape(nt, COMB_TM, TOP_K).transpose(0, 2, 1).reshape(nt, 1, TOP_K * COMB_TM)
    out = _combine(pos3, gate, x1, ln2_g[0].reshape(1, -1), ln2_b[0].reshape(1, -1), yb)
    return out.reshape(B, S, D)
```

```python
import numpy as np
import jax
import jax.numpy as jnp
from jax import lax
from jax.experimental import pallas as pl
from jax.experimental.pallas import tpu as pltpu

F32 = jnp.float32
BF16 = jnp.bfloat16

D_MODEL = 2048
HEAD_DIM = 128
ROT_DIM = HEAD_DIM // 4
ROPE_THETA = 500000.0
DIL_PATTERNS = ((128, 1), (512, 4), (2048, 16))
N_GROUPS = len(DIL_PATTERNS)
HEADS = 4
GROUP_W = HEADS * HEAD_DIM
ATTN_WIDTH = N_GROUPS * GROUP_W
CHUNK = 128
SGU_GROUPS = 8
SGU_WIDTH = SGU_GROUPS * CHUNK
IN_WIDTH = 3 * ATTN_WIDTH + 2 * SGU_WIDTH + 2 * D_MODEL
N_EXPERTS = 32
TOP_K = 4
D_EXPERT = D_MODEL
SWIGLU_LIMIT = 7.0
SWIGLU_ALPHA = 1.702
LN_EPS = 1e-5
DEPTH = 1
DEEPNORM_ALPHA = (2 * DEPTH) ** 0.25
NEG_INF = -1e30

LANES = 128
V7X_VMEM_BYTES = 64 * 1024 * 1024
VMEM_LIMIT = V7X_VMEM_BYTES - 8 * 1024 * 1024

COL_BLK = 512
PROJ_TM = 1024
PROJ_CHUNKS = 3
ATTN_BLK = 128
ATTN_QB = 4
ATTN_AHEAD = 10
TAIL_TM = 256
MOE_TM = 256
W_CHUNKS = 8
X_SLOTS = 3
COMB_TM = 128

Z_BLK0, GA_BLK0, GB_BLK0, Q_BLK0, K_BLK0, V_BLK0 = 0, 4, 8, 12, 15, 18
N_PROJ_BLKS = IN_WIDTH // COL_BLK
assert N_PROJ_BLKS == 21 and N_PROJ_BLKS % PROJ_CHUNKS == 0


def _cparams(sem):
    return pltpu.CompilerParams(dimension_semantics=sem, vmem_limit_bytes=VMEM_LIMIT)


def _layer_norm(v, g, b):
    mu = jnp.mean(v, axis=-1, keepdims=True)
    c = v - mu
    var = jnp.mean(c * c, axis=-1, keepdims=True)
    return c * lax.rsqrt(var + LN_EPS) * g + b


def _epilogue_kind(blk):
    if blk < GA_BLK0:
        return "gelu"
    if blk < Q_BLK0:
        return "sigmoid"
    if blk < V_BLK0:
        return "rope"
    return "plain"


def _inproj_kernel(x_ref, w_ref, rc_ref, ra_ref, rb_ref, o_ref, xb_ref):
    j = pl.program_id(1)

    @pl.when(j == 0)
    def _():
        xb_ref[...] = x_ref[...].astype(BF16)

    def epilogue(acc, kind):
        if kind == "gelu":
            return 0.5 * acc * (1.0 + lax.erf(acc * (2.0 ** -0.5)))
        if kind == "sigmoid":
            return 1.0 / (1.0 + jnp.exp(-acc))
        if kind == "rope":
            c = jnp.concatenate([rc_ref[...]] * HEADS, axis=1)
            a = jnp.concatenate([ra_ref[...]] * HEADS, axis=1)
            b = jnp.concatenate([rb_ref[...]] * HEADS, axis=1)
            half = ROT_DIM // 2
            up = pltpu.roll(acc, COL_BLK - half, axis=1)
            dn = pltpu.roll(acc, half, axis=1)
            return acc * c + up * a + dn * b
        return acc

    patterns = {}
    for t in range(N_PROJ_BLKS // PROJ_CHUNKS):
        kinds = tuple(_epilogue_kind(t * PROJ_CHUNKS + c) for c in range(PROJ_CHUNKS))
        patterns.setdefault(kinds, []).append(t)
    for kinds, tiles in patterns.items():
        cond = j == tiles[0]
        for t in tiles[1:]:
            cond = cond | (j == t)

        @pl.when(cond)
        def _(kinds=kinds):
            for c, kind in enumerate(kinds):
                cs = slice(c * COL_BLK, (c + 1) * COL_BLK)
                acc = jnp.dot(xb_ref[...], w_ref[:, cs], preferred_element_type=F32)
                o_ref[:, cs] = epilogue(acc, kind).astype(BF16)


def _inproj(x2d, w_cat, rc, ra, rb):
    S = x2d.shape[0]
    tn = PROJ_CHUNKS * COL_BLK
    return pl.pallas_call(
        _inproj_kernel,
        grid=(S // PROJ_TM, IN_WIDTH // tn),
        in_specs=[
            pl.BlockSpec((PROJ_TM, D_MODEL), lambda i, j: (i, 0)),
            pl.BlockSpec((D_MODEL, tn), lambda i, j: (0, j)),
            pl.BlockSpec((PROJ_TM, LANES), lambda i, j: (i, 0)),
            pl.BlockSpec((PROJ_TM, LANES), lambda i, j: (i, 0)),
            pl.BlockSpec((PROJ_TM, LANES), lambda i, j: (i, 0)),
        ],
        out_specs=pl.BlockSpec((PROJ_TM, tn), lambda i, j: (i, j)),
        out_shape=jax.ShapeDtypeStruct((S, IN_WIDTH), BF16),
        scratch_shapes=[pltpu.VMEM((PROJ_TM, D_MODEL), BF16)],
        compiler_params=_cparams(("arbitrary", "arbitrary")),
        name="inproj",
    )(x2d, w_cat, rc, ra, rb)


def _attn_kernel(q_ref, kp_ref, kc_ref, kn_ref, vp_ref, vc_ref, vn_ref, band_ref, o_ref, l_ref):
    i = pl.program_id(1)
    n = pl.num_programs(1)
    k = jnp.concatenate([kp_ref[0], kc_ref[0], kn_ref[0]], axis=0)
    v = jnp.concatenate([vp_ref[0], vc_ref[0], vn_ref[0]], axis=0)
    hq = ATTN_BLK // 2
    col = lax.broadcasted_iota(jnp.int32, (hq, 2 * ATTN_BLK), 1)
    lane = lax.broadcasted_iota(jnp.int32, (hq, LANES), 1)
    scale = HEAD_DIM ** -0.5
    first_bias = jnp.where(col >= jnp.where(i > 0, 0, ATTN_BLK), band_ref[0], NEG_INF)
    last_bias = jnp.where(col < jnp.where(i < n - 1, 2 * ATTN_BLK, ATTN_BLK), band_ref[1], NEG_INF)

    units = [(s, hf, h) for s in range(ATTN_QB) for hf in range(2) for h in range(HEADS)]

    def scores(s, hf, h):
        qs = slice(s * ATTN_BLK + hf * hq, s * ATTN_BLK + (hf + 1) * hq)
        ks = slice((s + hf) * ATTN_BLK, (s + hf + 2) * ATTN_BLK)
        hs = slice(h * HEAD_DIM, (h + 1) * HEAD_DIM)
        return lax.dot_general(q_ref[0, qs, hs], k[ks, hs], (((1,), (1,)), ((), ())),
                               preferred_element_type=F32)

    pending = [scores(*u) for u in units[:ATTN_AHEAD]]
    lse_tile = None
    for j, (s, hf, h) in enumerate(units):
        raw = pending.pop(0)
        if j + ATTN_AHEAD < len(units):
            pending.append(scores(*units[j + ATTN_AHEAD]))
        if s == 0 and hf == 0:
            bias = first_bias
        elif s == ATTN_QB - 1 and hf == 1:
            bias = last_bias
        else:
            bias = band_ref[hf]
        qs = slice(s * ATTN_BLK + hf * hq, s * ATTN_BLK + (hf + 1) * hq)
        ks = slice((s + hf) * ATTN_BLK, (s + hf + 2) * ATTN_BLK)
        hs = slice(h * HEAD_DIM, (h + 1) * HEAD_DIM)
        sc = raw * scale + bias
        m = jnp.max(sc, axis=-1, keepdims=True)
        p = jnp.exp(sc - m)
        den = jnp.sum(p, axis=-1, keepdims=True)
        o = jnp.dot(p.astype(BF16), v[ks, hs], preferred_element_type=F32) / den
        o_ref[0, qs, hs] = o.astype(BF16)
        if h == 0:
            lse_tile = jnp.zeros((hq, LANES), F32)
        lse_tile = jnp.where(lane == h, m + jnp.log(den), lse_tile)
        if h == HEADS - 1:
            l_ref[0, qs, :] = lse_tile


def _band_bias():
    hq = ATTN_BLK // 2
    out = np.empty((2, hq, 2 * ATTN_BLK), np.float32)
    for hf in range(2):
        i = np.arange(hq)[:, None] + hf * hq
        c = np.arange(2 * ATTN_BLK)[None, :] + hf * ATTN_BLK
        rel = c - ATTN_BLK - i
        out[hf] = np.where(np.abs(rel) <= hq, 0.0, NEG_INF)
    return jnp.asarray(out)


def _attention(arr, qb, kb, vb):
    d, L, _ = arr.shape
    assert ATTN_QB >= 2 and L % (ATTN_QB * ATTN_BLK) == 0
    nb = L // ATTN_BLK
    big = (1, ATTN_QB * ATTN_BLK, GROUP_W)
    small = (1, ATTN_BLK, GROUP_W)
    cur = lambda cb: pl.BlockSpec(big, lambda r, i: (r, i, cb))
    prev = lambda cb: pl.BlockSpec(small, lambda r, i: (r, jnp.maximum(ATTN_QB * i - 1, 0), cb))
    nxt = lambda cb: pl.BlockSpec(small, lambda r, i: (r, jnp.minimum(ATTN_QB * (i + 1), nb - 1), cb))
    return pl.pallas_call(
        _attn_kernel,
        grid=(d, nb // ATTN_QB),
        in_specs=[cur(qb), prev(kb), cur(kb), nxt(kb), prev(vb), cur(vb), nxt(vb),
                  pl.BlockSpec((2, ATTN_BLK // 2, 2 * ATTN_BLK), lambda r, i: (0, 0, 0))],
        out_specs=[pl.BlockSpec(big, lambda r, i: (r, i, 0)),
                   pl.BlockSpec((1, ATTN_QB * ATTN_BLK, LANES), lambda r, i: (r, i, 0))],
        out_shape=[jax.ShapeDtypeStruct((d, L, GROUP_W), BF16),
                   jax.ShapeDtypeStruct((d, L, LANES), F32)],
        compiler_params=_cparams(("arbitrary", "arbitrary")),
        name=f"attn_d{d}",
    )(arr, arr, arr, arr, arr, arr, arr, _band_bias())


def _tail_kernel(z_ref, ga_ref, gb_ref, o1_ref, o2_ref, o3_ref, l1_ref, l2_ref, l3_ref, x_ref,
                 sw_ref, sbias_ref, slg_ref, slb_ref, wso_ref, wao_ref, wo_ref,
                 g1_ref, b1_ref, wr_ref, br_ref, ltri_ref,
                 x1_ref, idx_ref, gate_ref, rank_ref, cnt_ref, carry_ref):
    tm = z_ref.shape[0]

    @pl.when(pl.program_id(0) == 0)
    def _():
        carry_ref[...] = jnp.zeros(carry_ref.shape, carry_ref.dtype)

    z = z_ref[...]
    u = z[:, :SGU_WIDTH].astype(F32)
    vn = _layer_norm(z[:, SGU_WIDTH:].astype(F32), slg_ref[...], slb_ref[...]).astype(BF16)
    rows = []
    for c in range(tm // CHUNK):
        cols = []
        for g in range(SGU_GROUPS):
            blk = vn[c * CHUNK:(c + 1) * CHUNK, g * CHUNK:(g + 1) * CHUNK]
            cols.append(jnp.dot(sw_ref[g], blk, preferred_element_type=F32))
        rows.append(jnp.concatenate(cols, axis=1) + sbias_ref[...])
    mixed = jnp.concatenate(rows, axis=0)
    y_b = jnp.dot((u * mixed).astype(BF16), wso_ref[...], preferred_element_type=F32)

    l1, l2, l3 = l1_ref[...], l2_ref[...], l3_ref[...]
    m = jnp.maximum(jnp.maximum(l1, l2), l3)
    e1, e2, e3 = jnp.exp(l1 - m), jnp.exp(l2 - m), jnp.exp(l3 - m)
    inv = 1.0 / (e1 + e2 + e3)
    w1, w2, w3 = e1 * inv, e2 * inv, e3 * inv
    parts = []
    for h in range(HEADS):
        hs = slice(h * HEAD_DIM, (h + 1) * HEAD_DIM)
        parts.append(w1[:, h:h + 1] * o1_ref[:, hs].astype(F32)
                     + w2[:, h:h + 1] * o2_ref[:, hs].astype(F32)
                     + w3[:, h:h + 1] * o3_ref[:, hs].astype(F32))
    o_a = jnp.concatenate(parts, axis=1).astype(BF16)
    y_a = jnp.dot(o_a, wao_ref[...], preferred_element_type=F32)

    merged = ga_ref[...].astype(F32) * y_a + gb_ref[...].astype(F32) * y_b
    hmix = jnp.dot(merged.astype(BF16), wo_ref[...], preferred_element_type=F32)
    x1 = _layer_norm(DEEPNORM_ALPHA * x_ref[...] + hmix, g1_ref[...], b1_ref[...])
    x1_ref[...] = x1

    logits = jnp.dot(x1.astype(BF16), wr_ref[...], preferred_element_type=F32) + br_ref[...]
    lane = lax.broadcasted_iota(jnp.int32, (tm, LANES), 1)
    lane_f = lane.astype(F32)
    vals = jnp.where(lane < N_EXPERTS, logits, -jnp.inf)
    idx_tile = jnp.zeros((tm, LANES), F32)
    val_tile = jnp.zeros((tm, LANES), F32)
    member = jnp.zeros((tm, LANES), F32)
    top0 = None
    picks = []
    for kk in range(TOP_K):
        mk = jnp.max(vals, axis=-1, keepdims=True)
        ik = jnp.min(jnp.where(vals == mk, lane_f, float(LANES)), axis=-1, keepdims=True)
        hit = lane_f == ik
        vals = jnp.where(hit, -jnp.inf, vals)
        member = jnp.where(hit, 1.0, member)
        picks.append(hit)
        if kk == 0:
            top0 = mk
        idx_tile = jnp.where(lane == kk, ik, idx_tile)
        val_tile = jnp.where(lane == kk, jnp.exp(mk - top0), val_tile)
    idx_ref[...] = idx_tile
    gate_ref[...] = val_tile / jnp.sum(val_tile, axis=-1, keepdims=True)

    before = jnp.dot(ltri_ref[...], member.astype(BF16), preferred_element_type=F32) + carry_ref[0:1, :]
    rank_tile = jnp.zeros((tm, LANES), F32)
    for kk in range(TOP_K):
        rk = jnp.sum(jnp.where(picks[kk], before, 0.0), axis=-1, keepdims=True)
        rank_tile = jnp.where(lane == kk, rk, rank_tile)
    rank_ref[...] = rank_tile
    carry_ref[...] = carry_ref[...] + jnp.sum(member, axis=0, keepdims=True)
    cnt_ref[...] = carry_ref[...]


def _tail(proj, o1, o2, o3, l1, l2, l3, x2d, sw, sbias, slg, slb, wso, wao, wo, g1, b1, wr, br):
    S = x2d.shape[0]
    tm = TAIL_TM
    row = lambda w: pl.BlockSpec((tm, w), lambda i: (i, 0))

    def const(shape):
        nd = len(shape)
        return pl.BlockSpec(shape, lambda i: (0,) * nd, pipeline_mode=pl.Buffered(1))

    wide = lambda cb: pl.BlockSpec((tm, D_MODEL), lambda i: (i, cb))
    r_, c_ = np.arange(tm)[:, None], np.arange(tm)[None, :]
    ltri = jnp.asarray(c_ < r_, BF16)
    return pl.pallas_call(
        _tail_kernel,
        grid=(S // tm,),
        in_specs=[wide(0), wide(1), wide(2),
                  row(GROUP_W), row(GROUP_W), row(GROUP_W), row(LANES), row(LANES), row(LANES),
                  row(D_MODEL),
                  const(sw.shape), const(sbias.shape), const(slg.shape), const(slb.shape),
                  const(wso.shape), const(wao.shape), const(wo.shape),
                  const(g1.shape), const(b1.shape), const(wr.shape), const(br.shape), const(ltri.shape)],
        out_specs=[row(D_MODEL), row(LANES), row(LANES), row(LANES),
                   pl.BlockSpec((8, LANES), lambda i: (0, 0))],
        out_shape=[jax.ShapeDtypeStruct((S, D_MODEL), F32),
                   jax.ShapeDtypeStruct((S, LANES), F32),
                   jax.ShapeDtypeStruct((S, LANES), F32),
                   jax.ShapeDtypeStruct((S, LANES), F32),
                   jax.ShapeDtypeStruct((8, LANES), F32)],
        scratch_shapes=[pltpu.VMEM((8, LANES), F32)],
        compiler_params=_cparams(("arbitrary",)),
        name="tail",
    )(proj, proj, proj, o1, o2, o3, l1, l2, l3, x2d, sw, sbias, slg, slb, wso, wao, wo, g1, b1, wr, br, ltri)


def _row_gather(idx_ref, n, src_hbm, dst, sem):
    def body(r, carry):
        t = idx_ref[0, 0, r]
        pltpu.make_async_copy(src_hbm.at[pl.ds(t, 1)], dst.at[pl.ds(r, 1)], sem).start()
        return carry
    lax.fori_loop(0, n, body, 0, unroll=8)


def _row_gather_unrolled(idx_ref, n, src_hbm, dst, sem):
    for r in range(n):
        t = idx_ref[0, 0, r]
        pltpu.make_async_copy(src_hbm.at[pl.ds(t, 1)], dst.at[pl.ds(r, 1)], sem).start()


def _stream_expert_weights(i, t0_ref, t1_ref, te_ref, nt_ref, w_hbm, stg, wsem, wbf):
    rows = stg.shape[1]
    ntask = nt_ref[0]

    def start(k):
        c = k % W_CHUNKS
        src = w_hbm.at[te_ref[k], pl.ds(pl.multiple_of(c * rows, rows), rows), :]
        pltpu.make_async_copy(src, stg.at[k % 2], wsem.at[k % 2]).start(priority=1)

    @pl.when(i == 0)
    def _():
        start(0)
        start(1)

    def body(k, carry):
        slot = k % 2
        pltpu.make_async_copy(w_hbm.at[0, pl.ds(0, rows), :], stg.at[slot], wsem.at[slot]).wait()
        par = (k // W_CHUNKS) % 2
        c = k % W_CHUNKS
        wbf[par, pl.ds(pl.multiple_of(c * rows, rows), rows), :] = stg[slot].astype(BF16)

        @pl.when(k + 2 < ntask)
        def _():
            start(k + 2)
        return carry

    lax.fori_loop(t0_ref[i], t1_ref[i], body, 0)


def _moe_up_kernel(be_ref, nu_ref, par_ref, t0_ref, t1_ref, te_ref, nt_ref,
                   tok_ref, tok1_ref, tok2_ref, x_hbm, w_hbm, b_ref, h_ref, xbuf, xsem, wbf, stg, wsem):
    i = pl.program_id(0)
    nused = nu_ref[0]
    slot = i % X_SLOTS

    @pl.when(i == 0)
    def _():
        _row_gather(tok_ref, MOE_TM, x_hbm, xbuf.at[0], xsem.at[0])

    @pl.when((i == 0) & (nused > 1))
    def _():
        _row_gather(tok1_ref, MOE_TM, x_hbm, xbuf.at[1], xsem.at[1])

    _stream_expert_weights(i, t0_ref, t1_ref, te_ref, nt_ref, w_hbm, stg, wsem, wbf)

    def compute(gather_ahead):
        if gather_ahead:
            ahead = (i + 2) % X_SLOTS
            _row_gather_unrolled(tok2_ref, MOE_TM, x_hbm, xbuf.at[ahead], xsem.at[ahead])
        pltpu.make_async_copy(x_hbm.at[pl.ds(0, MOE_TM)], xbuf.at[slot], xsem.at[slot]).wait()
        xb = xbuf[slot].astype(BF16)
        w = wbf.at[par_ref[i]]
        for c in range(D_EXPERT // COL_BLK):
            cs = slice(c * COL_BLK, (c + 1) * COL_BLK)
            us = slice(D_EXPERT + c * COL_BLK, D_EXPERT + (c + 1) * COL_BLK)
            g = jnp.dot(xb, w[:, cs], preferred_element_type=F32) + b_ref[0, :, cs]
            up = jnp.dot(xb, w[:, us], preferred_element_type=F32) + b_ref[0, :, us]
            g = jnp.minimum(g, SWIGLU_LIMIT)
            up = jnp.clip(up, -SWIGLU_LIMIT, SWIGLU_LIMIT)
            act = (up + 1.0) * (g * (1.0 / (1.0 + jnp.exp(-SWIGLU_ALPHA * g))))
            h_ref[:, cs] = act.astype(BF16)

    @pl.when(i + 2 < nused)
    def _():
        compute(True)

    @pl.when((i + 2 >= nused) & (i < nused))
    def _():
        compute(False)

    @pl.when(i >= nused)
    def _():
        h_ref[...] = jnp.zeros(h_ref.shape, h_ref.dtype)


def _moe_up(sched, row_tok3, x1, wgu, bgu):
    nblk = sched[0].shape[0]
    rows = D_MODEL // W_CHUNKS
    smem_blk = lambda off: pl.BlockSpec(
        (1, 1, MOE_TM), lambda i, *_: (jnp.minimum(i + off, nblk - 1), 0, 0), memory_space=pltpu.SMEM)
    grid_spec = pltpu.PrefetchScalarGridSpec(
        num_scalar_prefetch=len(sched),
        grid=(nblk,),
        in_specs=[smem_blk(0), smem_blk(1), smem_blk(2),
                  pl.BlockSpec(memory_space=pl.ANY),
                  pl.BlockSpec(memory_space=pl.ANY),
                  pl.BlockSpec((1, 1, 2 * D_EXPERT), lambda i, be, *_: (be[i], 0, 0))],
        out_specs=pl.BlockSpec((MOE_TM, D_EXPERT), lambda i, *_: (i, 0)),
        scratch_shapes=[pltpu.VMEM((X_SLOTS, MOE_TM, D_MODEL), F32), pltpu.SemaphoreType.DMA((X_SLOTS,)),
                        pltpu.VMEM((2, D_MODEL, 2 * D_EXPERT), BF16),
                        pltpu.VMEM((2, rows, 2 * D_EXPERT), F32), pltpu.SemaphoreType.DMA((2,))],
    )
    return pl.pallas_call(
        _moe_up_kernel,
        grid_spec=grid_spec,
        out_shape=jax.ShapeDtypeStruct((nblk * MOE_TM, D_EXPERT), BF16),
        compiler_params=_cparams(("arbitrary",)),
        name="moe_up",
    )(*sched, row_tok3, row_tok3, row_tok3, x1, wgu, bgu)


def _moe_down_kernel(be_ref, nu_ref, par_ref, t0_ref, t1_ref, te_ref, nt_ref,
                     h_ref, w_hbm, b_ref, y_ref, wbf, stg, wsem):
    i = pl.program_id(0)
    _stream_expert_weights(i, t0_ref, t1_ref, te_ref, nt_ref, w_hbm, stg, wsem, wbf)

    @pl.when(i < nu_ref[0])
    def _():
        y_ref[...] = jnp.dot(h_ref[...], wbf[par_ref[i]], preferred_element_type=F32) + b_ref[0]

    @pl.when(i >= nu_ref[0])
    def _():
        y_ref[...] = jnp.zeros(y_ref.shape, y_ref.dtype)


def _moe_down(sched, h, wd, bd):
    nblk = sched[0].shape[0]
    rows = D_EXPERT // W_CHUNKS
    grid_spec = pltpu.PrefetchScalarGridSpec(
        num_scalar_prefetch=len(sched),
        grid=(nblk,),
        in_specs=[pl.BlockSpec((MOE_TM, D_EXPERT), lambda i, be, nu, *_: (jnp.minimum(i, nu[0] - 1), 0)),
                  pl.BlockSpec(memory_space=pl.ANY),
                  pl.BlockSpec((1, 1, D_MODEL), lambda i, be, *_: (be[i], 0, 0))],
        out_specs=pl.BlockSpec((MOE_TM, D_MODEL), lambda i, *_: (i, 0)),
        scratch_shapes=[pltpu.VMEM((2, D_EXPERT, D_MODEL), BF16),
                        pltpu.VMEM((2, rows, D_MODEL), F32), pltpu.SemaphoreType.DMA((2,))],
    )
    return pl.pallas_call(
        _moe_down_kernel,
        grid_spec=grid_spec,
        out_shape=jax.ShapeDtypeStruct((nblk * MOE_TM, D_MODEL), F32),
        compiler_params=_cparams(("arbitrary",)),
        name="moe_down",
    )(*sched, h, wd, bd)


def _combine_kernel(pos_ref, posn_ref, gate_ref, x1_ref, g2_ref, b2_ref, y_hbm, o_ref, ybuf, sem):
    i = pl.program_id(0)
    n = pl.num_programs(0)
    slot = i % 2
    nrow = TOP_K * COMB_TM

    @pl.when(i == 0)
    def _():
        _row_gather(pos_ref, nrow, y_hbm, ybuf.at[0], sem.at[0])

    @pl.when(i + 1 < n)
    def _():
        _row_gather_unrolled(posn_ref, nrow, y_hbm, ybuf.at[1 - slot], sem.at[1 - slot])

    pltpu.make_async_copy(y_hbm.at[pl.ds(0, nrow)], ybuf.at[slot], sem.at[slot]).wait()
    gate = gate_ref[...]
    y = jnp.zeros((COMB_TM, D_MODEL), F32)
    for kk in range(TOP_K):
        y = y + gate[:, kk:kk + 1] * ybuf[slot, kk * COMB_TM:(kk + 1) * COMB_TM, :]
    o_ref[...] = _layer_norm(DEEPNORM_ALPHA * x1_ref[...] + y, g2_ref[...], b2_ref[...])


def _combine(pos3, gate, x1, g2, b2, yb):
    S = x1.shape[0]
    nt = S // COMB_TM
    nrow = TOP_K * COMB_TM
    smem_blk = lambda off: pl.BlockSpec(
        (1, 1, nrow), lambda i: (jnp.minimum(i + off, nt - 1), 0, 0), memory_space=pltpu.SMEM)
    return pl.pallas_call(
        _combine_kernel,
        grid=(nt,),
        in_specs=[smem_blk(0), smem_blk(1),
                  pl.BlockSpec((COMB_TM, LANES), lambda i: (i, 0)),
                  pl.BlockSpec((COMB_TM, D_MODEL), lambda i: (i, 0)),
                  pl.BlockSpec((1, D_MODEL), lambda i: (0, 0)),
                  pl.BlockSpec((1, D_MODEL), lambda i: (0, 0)),
                  pl.BlockSpec(memory_space=pl.ANY)],
        out_specs=pl.BlockSpec((COMB_TM, D_MODEL), lambda i: (i, 0)),
        out_shape=jax.ShapeDtypeStruct((S, D_MODEL), F32),
        scratch_shapes=[pltpu.VMEM((2, nrow, D_MODEL), F32), pltpu.SemaphoreType.DMA((2,))],
        compiler_params=_cparams(("arbitrary",)),
        name="combine",
    )(pos3, pos3, gate, x1, g2, b2, yb)


def _rope_tables(pos):
    half = ROT_DIM // 2
    inv_freq = ROPE_THETA ** (-jnp.arange(0, ROT_DIM, 2, dtype=F32) / ROT_DIM)
    ang = pos.astype(F32)[:, None] * inv_freq
    cos, sin = jnp.cos(ang), jnp.sin(ang)
    S = pos.shape[0]
    one = jnp.ones((S, HEAD_DIM - ROT_DIM), F32)
    zero = jnp.zeros((S, HEAD_DIM - ROT_DIM), F32)
    z16 = jnp.zeros((S, half), F32)
    rc = jnp.concatenate([cos, cos, one], axis=1)
    ra = jnp.concatenate([-sin, z16, zero], axis=1)
    rb = jnp.concatenate([z16, sin, zero], axis=1)
    return rc, ra, rb


def _routing(idx, rank, counts, S):
    n = S * TOP_K
    nblk = n // MOE_TM + N_EXPERTS
    onehot = (idx[:, :, None] == jnp.arange(N_EXPERTS, dtype=jnp.int32)[None, None, :])
    padded = (counts + MOE_TM - 1) // MOE_TM * MOE_TM
    pends = jnp.cumsum(padded)
    pstarts = pends - padded
    dest = jnp.sum(jnp.where(onehot, pstarts[None, None, :], 0), axis=-1) + rank
    tok = jnp.broadcast_to(jnp.arange(S, dtype=jnp.int32)[:, None], (S, TOP_K))
    row_tok = jnp.zeros((nblk * MOE_TM,), jnp.int32).at[dest.reshape(-1)].set(tok.reshape(-1))
    starts = jnp.arange(nblk, dtype=jnp.int32) * MOE_TM
    block_e = jnp.minimum(jnp.sum(pends[None, :] <= starts[:, None], axis=1, dtype=jnp.int32), N_EXPERTS - 1)
    nused = (pends[-1:] // MOE_TM).astype(jnp.int32)

    e_ids = jnp.arange(N_EXPERTS, dtype=jnp.int32)
    nb_e = (padded // MOE_TM).astype(jnp.int32)
    bstart_e = (pstarts // MOE_TM).astype(jnp.int32)
    nonempty = nb_e > 0
    run_of_e = jnp.cumsum(nonempty.astype(jnp.int32)) - 1
    nruns = jnp.sum(nonempty.astype(jnp.int32))
    run_e = jnp.sum(jnp.where(nonempty[None, :] & (run_of_e[None, :] == e_ids[:, None]), e_ids[None, :], 0), axis=1)
    task_e = jnp.repeat(run_e, W_CHUNKS).astype(jnp.int32)
    ntask = (W_CHUNKS * nruns).reshape(1).astype(jnp.int32)
    blk = jnp.arange(nblk, dtype=jnp.int32)
    used = blk < nused[0]
    r_i = run_of_e[block_e]
    j_i = blk - bstart_e[block_e]
    n_i = jnp.maximum(nb_e[block_e], 1)
    t0 = W_CHUNKS * (r_i + 1) + (W_CHUNKS * j_i) // n_i
    t1 = W_CHUNKS * (r_i + 1) + (W_CHUNKS * (j_i + 1)) // n_i
    t0 = jnp.where(blk == 0, 0, jnp.where(used, jnp.minimum(t0, ntask[0]), ntask[0])).astype(jnp.int32)
    t1 = jnp.where(used, jnp.minimum(t1, ntask[0]), ntask[0]).astype(jnp.int32)
    par = (r_i % 2).astype(jnp.int32)
    sched = (block_e, nused, par, t0, t1, task_e, ntask)
    return dest.astype(jnp.int32), row_tok, sched


def kernel(x, positions, w_in, w_attn_out, sgu_ln_g, sgu_ln_b, sgu_w, sgu_b, w_sgu_out, w_out,
           ln1_g, ln1_b, w_router, b_router, w_gate_up, b_gate_up, w_down, b_down, ln2_g, ln2_b):
    B, S, D = x.shape
    assert B == 1 and D == D_MODEL and w_in.shape[0] == DEPTH
    assert S % PROJ_TM == 0 and S % (DIL_PATTERNS[-1][1] * ATTN_QB * ATTN_BLK) == 0
    x2d = x.reshape(S, D)

    a0, a1, a2 = ATTN_WIDTH, 2 * ATTN_WIDTH, 3 * ATTN_WIDTH
    z1 = a2 + 2 * SGU_WIDTH
    wi = w_in[0]
    w_cat = jnp.concatenate([wi[:, a2:z1], wi[:, z1:z1 + D], wi[:, z1 + D:],
                             wi[:, :a0], wi[:, a0:a1], wi[:, a1:a2]], axis=1).astype(BF16)
    rc, ra, rb = _rope_tables(positions[0])

    proj = _inproj(x2d, w_cat, rc, ra, rb)

    outs, lses = [], []
    for gi, (window, d) in enumerate(DIL_PATTERNS):
        assert window // (2 * d) == ATTN_BLK // 2
        L = S // d
        if d == 1:
            o, l = _attention(proj.reshape(1, S, IN_WIDTH), Q_BLK0 + gi, K_BLK0 + gi, V_BLK0 + gi)
        else:
            cols = [proj[:, (b0 + gi) * GROUP_W:(b0 + gi + 1) * GROUP_W] for b0 in (Q_BLK0, K_BLK0, V_BLK0)]
            qkv = jnp.concatenate(cols, axis=1).reshape(L, d, 3 * GROUP_W).transpose(1, 0, 2)
            o, l = _attention(qkv, 0, 1, 2)
        outs.append(o.transpose(1, 0, 2).reshape(S, GROUP_W))
        lses.append(l.transpose(1, 0, 2).reshape(S, LANES))

    sbias = jnp.repeat(sgu_b[0].T, CHUNK, axis=1)
    wr = jnp.pad(w_router[0], ((0, 0), (0, LANES - N_EXPERTS))).astype(BF16)
    br = jnp.pad(b_router[0], (0, LANES - N_EXPERTS)).reshape(1, LANES)
    x1, idx_f, gate, rank_f, cnt_f = _tail(
        proj, outs[0], outs[1], outs[2], lses[0], lses[1], lses[2], x2d,
        sgu_w[0].astype(BF16), sbias, sgu_ln_g[0].reshape(1, -1), sgu_ln_b[0].reshape(1, -1),
        w_sgu_out[0].astype(BF16), w_attn_out[0].astype(BF16), w_out[0].astype(BF16),
        ln1_g[0].reshape(1, -1), ln1_b[0].reshape(1, -1), wr, br)

    idx = idx_f[:, :TOP_K].astype(jnp.int32)
    rank = rank_f[:, :TOP_K].astype(jnp.int32)
    counts = cnt_f[0, :N_EXPERTS].astype(jnp.int32)
    dest, row_tok, sched = _routing(idx, rank, counts, S)
    nblk = sched[0].shape[0]

    h = _moe_up(sched, row_tok.reshape(nblk, 1, MOE_TM), x1, w_gate_up[0], b_gate_up[0].reshape(N_EXPERTS, 1, -1))
    yb = _moe_down(sched, h, w_down[0], b_down[0].reshape(N_EXPERTS, 1, -1))

    nt = S // COMB_TM
    pos3 = dest.reshape(nt, COMB_TM, TOP_K).transpose(0, 2, 1).reshape(nt, 1, TOP_K * COMB_TM)
    out = _combine(pos3, gate, x1, ln2_g[0].reshape(1, -1), ln2_b[0].reshape(1, -1), yb)
    return out.reshape(B, S, D)
```

```python
import numpy as np
import jax
import jax.numpy as jnp
from jax import lax
from jax.experimental import pallas as pl
from jax.experimental.pallas import tpu as pltpu

F32 = jnp.float32
BF16 = jnp.bfloat16

D_MODEL = 2048
HEAD_DIM = 128
ROT_DIM = HEAD_DIM // 4
ROPE_THETA = 500000.0
DIL_PATTERNS = ((128, 1), (512, 4), (2048, 16))
N_GROUPS = len(DIL_PATTERNS)
HEADS = 4
GROUP_W = HEADS * HEAD_DIM
ATTN_WIDTH = N_GROUPS * GROUP_W
CHUNK = 128
SGU_GROUPS = 8
SGU_WIDTH = SGU_GROUPS * CHUNK
IN_WIDTH = 3 * ATTN_WIDTH + 2 * SGU_WIDTH + 2 * D_MODEL
N_EXPERTS = 32
TOP_K = 4
D_EXPERT = D_MODEL
SWIGLU_LIMIT = 7.0
SWIGLU_ALPHA = 1.702
LN_EPS = 1e-5
DEPTH = 1
DEEPNORM_ALPHA = (2 * DEPTH) ** 0.25
NEG_INF = -1e30

LANES = 128
V7X_VMEM_BYTES = 64 * 1024 * 1024
VMEM_LIMIT = V7X_VMEM_BYTES - 8 * 1024 * 1024

COL_BLK = 512
PROJ_TM = 512
PROJ_CHUNKS = 3
ATTN_BLK = 128
ATTN_QB = 4
ATTN_AHEAD = 10
TAIL_TM = 256
MOE_TM = 256
W_CHUNKS = 8
X_SLOTS = 3
COMB_TM = 128

Z_BLK0, GA_BLK0, GB_BLK0, QKV_BLK0 = 0, 4, 8, 12
N_PROJ_BLKS = IN_WIDTH // COL_BLK
MAIN_TILES = 5
assert N_PROJ_BLKS == 21 and N_PROJ_BLKS == (MAIN_TILES + 2) * PROJ_CHUNKS and QKV_BLK0 == (MAIN_TILES - 1) * PROJ_CHUNKS


def _cparams(sem):
    return pltpu.CompilerParams(dimension_semantics=sem, vmem_limit_bytes=VMEM_LIMIT)


def _layer_norm(v, g, b):
    mu = jnp.mean(v, axis=-1, keepdims=True)
    c = v - mu
    var = jnp.mean(c * c, axis=-1, keepdims=True)
    return c * lax.rsqrt(var + LN_EPS) * g + b


def _epilogue_kind(blk):
    if blk < GA_BLK0:
        return "gelu"
    if blk < QKV_BLK0:
        return "sigmoid"
    return "plain" if (blk - QKV_BLK0) % 3 == 2 else "rope"


def _inproj_kernel(x_ref, w_ref, rc_ref, ra_ref, rb_ref, o_ref, g2_ref, g3_ref, xb_ref, racc_ref):
    j = pl.program_id(1)
    tm = x_ref.shape[0]

    def store(c, val, residue_major):
        cs = slice(c * COL_BLK, (c + 1) * COL_BLK)
        if residue_major is None:
            o_ref[:, cs] = val.astype(BF16)
            return
        ref, d = residue_major
        for k in range(COL_BLK // LANES):
            racc_ref[k] = val[:, k * LANES:(k + 1) * LANES]
        for r in range(d):
            parts = [racc_ref[k, pl.ds(r, tm // d, stride=d), :] for k in range(COL_BLK // LANES)]
            ref[r, :, cs] = jnp.concatenate(parts, axis=1).astype(BF16)

    @pl.when(j == 0)
    def _():
        xb_ref[...] = x_ref[...].astype(BF16)

    def epilogue(acc, kind):
        if kind == "gelu":
            return 0.5 * acc * (1.0 + lax.erf(acc * (2.0 ** -0.5)))
        if kind == "sigmoid":
            return 1.0 / (1.0 + jnp.exp(-acc))
        if kind == "rope":
            c = jnp.concatenate([rc_ref[...]] * HEADS, axis=1)
            a = jnp.concatenate([ra_ref[...]] * HEADS, axis=1)
            b = jnp.concatenate([rb_ref[...]] * HEADS, axis=1)
            half = ROT_DIM // 2
            up = pltpu.roll(acc, COL_BLK - half, axis=1)
            dn = pltpu.roll(acc, half, axis=1)
            return acc * c + up * a + dn * b
        return acc

    dests = {MAIN_TILES: (g2_ref, DIL_PATTERNS[1][1]), MAIN_TILES + 1: (g3_ref, DIL_PATTERNS[2][1])}
    patterns = {}
    for t in range(N_PROJ_BLKS // PROJ_CHUNKS):
        kinds = tuple(_epilogue_kind(t * PROJ_CHUNKS + c) for c in range(PROJ_CHUNKS))
        patterns.setdefault((kinds, t if t in dests else None), []).append(t)
    for (kinds, special), tiles in patterns.items():
        cond = j == tiles[0]
        for t in tiles[1:]:
            cond = cond | (j == t)

        @pl.when(cond)
        def _(kinds=kinds, special=special):
            for c, kind in enumerate(kinds):
                cs = slice(c * COL_BLK, (c + 1) * COL_BLK)
                acc = jnp.dot(xb_ref[...], w_ref[:, cs], preferred_element_type=F32)
                store(c, epilogue(acc, kind), dests.get(special))


def _inproj(x2d, w_cat, rc, ra, rb):
    S = x2d.shape[0]
    tm = PROJ_TM
    tn = PROJ_CHUNKS * COL_BLK
    d2, d3 = DIL_PATTERNS[1][1], DIL_PATTERNS[2][1]
    return pl.pallas_call(
        _inproj_kernel,
        grid=(S // tm, IN_WIDTH // tn),
        in_specs=[
            pl.BlockSpec((tm, D_MODEL), lambda i, j: (i, 0)),
            pl.BlockSpec((D_MODEL, tn), lambda i, j: (0, j)),
            pl.BlockSpec((tm, LANES), lambda i, j: (i, 0)),
            pl.BlockSpec((tm, LANES), lambda i, j: (i, 0)),
            pl.BlockSpec((tm, LANES), lambda i, j: (i, 0)),
        ],
        out_specs=[pl.BlockSpec((tm, tn), lambda i, j: (i, jnp.minimum(j, MAIN_TILES - 1))),
                   pl.BlockSpec((d2, tm // d2, tn), lambda i, j: (0, i, 0)),
                   pl.BlockSpec((d3, tm // d3, tn), lambda i, j: (0, i, 0))],
        out_shape=[jax.ShapeDtypeStruct((S, MAIN_TILES * tn), BF16),
                   jax.ShapeDtypeStruct((d2, S // d2, tn), BF16),
                   jax.ShapeDtypeStruct((d3, S // d3, tn), BF16)],
        scratch_shapes=[pltpu.VMEM((tm, D_MODEL), BF16),
                        pltpu.VMEM((COL_BLK // LANES, tm, LANES), F32)],
        compiler_params=_cparams(("arbitrary", "arbitrary")),
        name="inproj",
    )(x2d, w_cat, rc, ra, rb)


def _attn_kernel(q_ref, kp_ref, kc_ref, kn_ref, vp_ref, vc_ref, vn_ref, band_ref, o_ref, l_ref):
    i = pl.program_id(1)
    n = pl.num_programs(1)
    k = jnp.concatenate([kp_ref[0], kc_ref[0], kn_ref[0]], axis=0)
    v = jnp.concatenate([vp_ref[0], vc_ref[0], vn_ref[0]], axis=0)
    hq = ATTN_BLK // 2
    col = lax.broadcasted_iota(jnp.int32, (hq, 2 * ATTN_BLK), 1)
    lane = lax.broadcasted_iota(jnp.int32, (hq, LANES), 1)
    scale = HEAD_DIM ** -0.5
    first_bias = jnp.where(col >= jnp.where(i > 0, 0, ATTN_BLK), band_ref[0], NEG_INF)
    last_bias = jnp.where(col < jnp.where(i < n - 1, 2 * ATTN_BLK, ATTN_BLK), band_ref[1], NEG_INF)

    units = [(s, hf, h) for s in range(ATTN_QB) for hf in range(2) for h in range(HEADS)]

    def scores(s, hf, h):
        qs = slice(s * ATTN_BLK + hf * hq, s * ATTN_BLK + (hf + 1) * hq)
        ks = slice((s + hf) * ATTN_BLK, (s + hf + 2) * ATTN_BLK)
        hs = slice(h * HEAD_DIM, (h + 1) * HEAD_DIM)
        return lax.dot_general(q_ref[0, qs, hs], k[ks, hs], (((1,), (1,)), ((), ())),
                               preferred_element_type=F32)

    pending = [scores(*u) for u in units[:ATTN_AHEAD]]
    lse_tile = None
    for j, (s, hf, h) in enumerate(units):
        raw = pending.pop(0)
        if j + ATTN_AHEAD < len(units):
            pending.append(scores(*units[j + ATTN_AHEAD]))
        if s == 0 and hf == 0:
            bias = first_bias
        elif s == ATTN_QB - 1 and hf == 1:
            bias = last_bias
        else:
            bias = band_ref[hf]
        qs = slice(s * ATTN_BLK + hf * hq, s * ATTN_BLK + (hf + 1) * hq)
        ks = slice((s + hf) * ATTN_BLK, (s + hf + 2) * ATTN_BLK)
        hs = slice(h * HEAD_DIM, (h + 1) * HEAD_DIM)
        sc = raw * scale + bias
        m = jnp.max(sc, axis=-1, keepdims=True)
        p = jnp.exp(sc - m)
        den = jnp.sum(p, axis=-1, keepdims=True)
        o = jnp.dot(p.astype(BF16), v[ks, hs], preferred_element_type=F32) / den
        o_ref[0, qs, hs] = o.astype(BF16)
        if h == 0:
            lse_tile = jnp.zeros((hq, LANES), F32)
        lse_tile = jnp.where(lane == h, m + jnp.log(den), lse_tile)
        if h == HEADS - 1:
            l_ref[0, qs, :] = lse_tile


def _band_bias():
    hq = ATTN_BLK // 2
    out = np.empty((2, hq, 2 * ATTN_BLK), np.float32)
    for hf in range(2):
        i = np.arange(hq)[:, None] + hf * hq
        c = np.arange(2 * ATTN_BLK)[None, :] + hf * ATTN_BLK
        rel = c - ATTN_BLK - i
        out[hf] = np.where(np.abs(rel) <= hq, 0.0, NEG_INF)
    return jnp.asarray(out)


def _attention(arr, qb, kb, vb):
    d, L, _ = arr.shape
    assert ATTN_QB >= 2 and L % (ATTN_QB * ATTN_BLK) == 0
    nb = L // ATTN_BLK
    big = (1, ATTN_QB * ATTN_BLK, GROUP_W)
    small = (1, ATTN_BLK, GROUP_W)
    cur = lambda cb: pl.BlockSpec(big, lambda r, i: (r, i, cb))
    prev = lambda cb: pl.BlockSpec(small, lambda r, i: (r, jnp.maximum(ATTN_QB * i - 1, 0), cb))
    nxt = lambda cb: pl.BlockSpec(small, lambda r, i: (r, jnp.minimum(ATTN_QB * (i + 1), nb - 1), cb))
    return pl.pallas_call(
        _attn_kernel,
        grid=(d, nb // ATTN_QB),
        in_specs=[cur(qb), prev(kb), cur(kb), nxt(kb), prev(vb), cur(vb), nxt(vb),
                  pl.BlockSpec((2, ATTN_BLK // 2, 2 * ATTN_BLK), lambda r, i: (0, 0, 0))],
        out_specs=[pl.BlockSpec(big, lambda r, i: (r, i, 0)),
                   pl.BlockSpec((1, ATTN_QB * ATTN_BLK, LANES), lambda r, i: (r, i, 0))],
        out_shape=[jax.ShapeDtypeStruct((d, L, GROUP_W), BF16),
                   jax.ShapeDtypeStruct((d, L, LANES), F32)],
        compiler_params=_cparams(("arbitrary", "arbitrary")),
        name=f"attn_d{d}",
    )(arr, arr, arr, arr, arr, arr, arr, _band_bias())


def _tail_kernel(z_ref, ga_ref, gb_ref, o1_ref, o2_ref, o3_ref, l1_ref, l2_ref, l3_ref, x_ref,
                 sw_ref, sbias_ref, slg_ref, slb_ref, wso_ref, wao_ref, wo_ref,
                 g1_ref, b1_ref, wr_ref, br_ref, ltri_ref,
                 x1_ref, idx_ref, gate_ref, rank_ref, cnt_ref,
                 carry_ref, on2_ref, ln2_ref, on3_ref, ln3_ref):
    tm = z_ref.shape[0]

    @pl.when(pl.program_id(0) == 0)
    def _():
        carry_ref[...] = jnp.zeros(carry_ref.shape, carry_ref.dtype)

    z = z_ref[...]
    u = z[:, :SGU_WIDTH].astype(F32)
    vn = _layer_norm(z[:, SGU_WIDTH:].astype(F32), slg_ref[...], slb_ref[...]).astype(BF16)
    rows = []
    for c in range(tm // CHUNK):
        cols = []
        for g in range(SGU_GROUPS):
            blk = vn[c * CHUNK:(c + 1) * CHUNK, g * CHUNK:(g + 1) * CHUNK]
            cols.append(jnp.dot(sw_ref[g], blk, preferred_element_type=F32))
        rows.append(jnp.concatenate(cols, axis=1) + sbias_ref[...])
    mixed = jnp.concatenate(rows, axis=0)
    y_b = jnp.dot((u * mixed).astype(BF16), wso_ref[...], preferred_element_type=F32)

    def to_position_order(o_ref, l_ref, onat_ref, lnat_ref):
        d = o_ref.shape[0]
        for r in range(d):
            rows = pl.ds(r, tm // d, stride=d)
            for h in range(HEADS):
                onat_ref[h, rows, :] = o_ref[r, :, h * HEAD_DIM:(h + 1) * HEAD_DIM].astype(F32)
            lnat_ref[rows, :] = l_ref[r]

    to_position_order(o2_ref, l2_ref, on2_ref, ln2_ref)
    to_position_order(o3_ref, l3_ref, on3_ref, ln3_ref)

    l1, l2, l3 = l1_ref[...], ln2_ref[...], ln3_ref[...]
    m = jnp.maximum(jnp.maximum(l1, l2), l3)
    e1, e2, e3 = jnp.exp(l1 - m), jnp.exp(l2 - m), jnp.exp(l3 - m)
    inv = 1.0 / (e1 + e2 + e3)
    w1, w2, w3 = e1 * inv, e2 * inv, e3 * inv
    parts = []
    for h in range(HEADS):
        hs = slice(h * HEAD_DIM, (h + 1) * HEAD_DIM)
        parts.append(w1[:, h:h + 1] * o1_ref[:, hs].astype(F32)
                     + w2[:, h:h + 1] * on2_ref[h]
                     + w3[:, h:h + 1] * on3_ref[h])
    o_a = jnp.concatenate(parts, axis=1).astype(BF16)
    y_a = jnp.dot(o_a, wao_ref[...], preferred_element_type=F32)

    merged = ga_ref[...].astype(F32) * y_a + gb_ref[...].astype(F32) * y_b
    hmix = jnp.dot(merged.astype(BF16), wo_ref[...], preferred_element_type=F32)
    x1 = _layer_norm(DEEPNORM_ALPHA * x_ref[...] + hmix, g1_ref[...], b1_ref[...])
    x1_ref[...] = x1

    logits = jnp.dot(x1.astype(BF16), wr_ref[...], preferred_element_type=F32) + br_ref[...]
    lane = lax.broadcasted_iota(jnp.int32, (tm, LANES), 1)
    lane_f = lane.astype(F32)
    vals = jnp.where(lane < N_EXPERTS, logits, -jnp.inf)
    idx_tile = jnp.zeros((tm, LANES), F32)
    val_tile = jnp.zeros((tm, LANES), F32)
    member = jnp.zeros((tm, LANES), F32)
    top0 = None
    picks = []
    for kk in range(TOP_K):
        mk = jnp.max(vals, axis=-1, keepdims=True)
        ik = jnp.min(jnp.where(vals == mk, lane_f, float(LANES)), axis=-1, keepdims=True)
        hit = lane_f == ik
        vals = jnp.where(hit, -jnp.inf, vals)
        member = jnp.where(hit, 1.0, member)
        picks.append(hit)
        if kk == 0:
            top0 = mk
        idx_tile = jnp.where(lane == kk, ik, idx_tile)
        val_tile = jnp.where(lane == kk, jnp.exp(mk - top0), val_tile)
    idx_ref[...] = idx_tile
    gate_ref[...] = val_tile / jnp.sum(val_tile, axis=-1, keepdims=True)

    before = jnp.dot(ltri_ref[...], member.astype(BF16), preferred_element_type=F32) + carry_ref[0:1, :]
    rank_tile = jnp.zeros((tm, LANES), F32)
    for kk in range(TOP_K):
        rk = jnp.sum(jnp.where(picks[kk], before, 0.0), axis=-1, keepdims=True)
        rank_tile = jnp.where(lane == kk, rk, rank_tile)
    rank_ref[...] = rank_tile
    carry_ref[...] = carry_ref[...] + jnp.sum(member, axis=0, keepdims=True)
    cnt_ref[...] = carry_ref[...]


def _tail(proj, o1, o2, o3, l1, l2, l3, x2d, sw, sbias, slg, slb, wso, wao, wo, g1, b1, wr, br):
    S = x2d.shape[0]
    tm = TAIL_TM
    row = lambda w: pl.BlockSpec((tm, w), lambda i: (i, 0))

    def const(shape):
        nd = len(shape)
        return pl.BlockSpec(shape, lambda i: (0,) * nd, pipeline_mode=pl.Buffered(1))

    wide = lambda cb: pl.BlockSpec((tm, D_MODEL), lambda i: (i, cb))
    res = lambda a: pl.BlockSpec((a.shape[0], tm // a.shape[0], a.shape[2]), lambda i: (0, i, 0))
    r_, c_ = np.arange(tm)[:, None], np.arange(tm)[None, :]
    ltri = jnp.asarray(c_ < r_, BF16)
    return pl.pallas_call(
        _tail_kernel,
        grid=(S // tm,),
        in_specs=[wide(0), wide(1), wide(2),
                  row(GROUP_W), res(o2), res(o3), row(LANES), res(l2), res(l3),
                  row(D_MODEL),
                  const(sw.shape), const(sbias.shape), const(slg.shape), const(slb.shape),
                  const(wso.shape), const(wao.shape), const(wo.shape),
                  const(g1.shape), const(b1.shape), const(wr.shape), const(br.shape), const(ltri.shape)],
        out_specs=[row(D_MODEL), row(LANES), row(LANES), row(LANES),
                   pl.BlockSpec((8, LANES), lambda i: (0, 0))],
        out_shape=[jax.ShapeDtypeStruct((S, D_MODEL), F32),
                   jax.ShapeDtypeStruct((S, LANES), F32),
                   jax.ShapeDtypeStruct((S, LANES), F32),
                   jax.ShapeDtypeStruct((S, LANES), F32),
                   jax.ShapeDtypeStruct((8, LANES), F32)],
        scratch_shapes=[pltpu.VMEM((8, LANES), F32),
                        pltpu.VMEM((HEADS, tm, HEAD_DIM), F32), pltpu.VMEM((tm, LANES), F32),
                        pltpu.VMEM((HEADS, tm, HEAD_DIM), F32), pltpu.VMEM((tm, LANES), F32)],
        compiler_params=_cparams(("arbitrary",)),
        name="tail",
    )(proj, proj, proj, o1, o2, o3, l1, l2, l3, x2d, sw, sbias, slg, slb, wso, wao, wo, g1, b1, wr, br, ltri)


def _row_gather(idx_ref, n, src_hbm, dst, sem):
    def body(r, carry):
        t = idx_ref[0, 0, r]
        pltpu.make_async_copy(src_hbm.at[pl.ds(t, 1)], dst.at[pl.ds(r, 1)], sem).start()
        return carry
    lax.fori_loop(0, n, body, 0, unroll=8)


def _row_gather_unrolled(idx_ref, n, src_hbm, dst, sem):
    for r in range(n):
        t = idx_ref[0, 0, r]
        pltpu.make_async_copy(src_hbm.at[pl.ds(t, 1)], dst.at[pl.ds(r, 1)], sem).start()


def _stream_expert_weights(i, t0_ref, t1_ref, te_ref, nt_ref, w_hbm, stg, wsem, wbf):
    rows = stg.shape[1]
    ntask = nt_ref[0]

    def start(k):
        c = k % W_CHUNKS
        src = w_hbm.at[te_ref[k], pl.ds(pl.multiple_of(c * rows, rows), rows), :]
        pltpu.make_async_copy(src, stg.at[k % 2], wsem.at[k % 2]).start(priority=1)

    @pl.when(i == 0)
    def _():
        start(0)
        start(1)

    def body(k, carry):
        slot = k % 2
        pltpu.make_async_copy(w_hbm.at[0, pl.ds(0, rows), :], stg.at[slot], wsem.at[slot]).wait()
        par = (k // W_CHUNKS) % 2
        c = k % W_CHUNKS
        wbf[par, pl.ds(pl.multiple_of(c * rows, rows), rows), :] = stg[slot].astype(BF16)

        @pl.when(k + 2 < ntask)
        def _():
            start(k + 2)
        return carry

    lax.fori_loop(t0_ref[i], t1_ref[i], body, 0)


def _moe_up_kernel(be_ref, nu_ref, par_ref, t0_ref, t1_ref, te_ref, nt_ref,
                   tok_ref, tok1_ref, tok2_ref, x_hbm, w_hbm, b_ref, h_ref, xbuf, xsem, wbf, stg, wsem):
    i = pl.program_id(0)
    nused = nu_ref[0]
    slot = i % X_SLOTS

    @pl.when(i == 0)
    def _():
        _row_gather(tok_ref, MOE_TM, x_hbm, xbuf.at[0], xsem.at[0])

    @pl.when((i == 0) & (nused > 1))
    def _():
        _row_gather(tok1_ref, MOE_TM, x_hbm, xbuf.at[1], xsem.at[1])

    _stream_expert_weights(i, t0_ref, t1_ref, te_ref, nt_ref, w_hbm, stg, wsem, wbf)

    def compute(gather_ahead):
        if gather_ahead:
            ahead = (i + 2) % X_SLOTS
            _row_gather_unrolled(tok2_ref, MOE_TM, x_hbm, xbuf.at[ahead], xsem.at[ahead])
        pltpu.make_async_copy(x_hbm.at[pl.ds(0, MOE_TM)], xbuf.at[slot], xsem.at[slot]).wait()
        xb = xbuf[slot].astype(BF16)
        w = wbf.at[par_ref[i]]
        for c in range(D_EXPERT // COL_BLK):
            cs = slice(c * COL_BLK, (c + 1) * COL_BLK)
            us = slice(D_EXPERT + c * COL_BLK, D_EXPERT + (c + 1) * COL_BLK)
            g = jnp.dot(xb, w[:, cs], preferred_element_type=F32) + b_ref[0, :, cs]
            up = jnp.dot(xb, w[:, us], preferred_element_type=F32) + b_ref[0, :, us]
            g = jnp.minimum(g, SWIGLU_LIMIT)
            up = jnp.clip(up, -SWIGLU_LIMIT, SWIGLU_LIMIT)
            act = (up + 1.0) * (g * (1.0 / (1.0 + jnp.exp(-SWIGLU_ALPHA * g))))
            h_ref[:, cs] = act.astype(BF16)

    @pl.when(i + 2 < nused)
    def _():
        compute(True)

    @pl.when((i + 2 >= nused) & (i < nused))
    def _():
        compute(False)

    @pl.when(i >= nused)
    def _():
        h_ref[...] = jnp.zeros(h_ref.shape, h_ref.dtype)


def _moe_up(sched, row_tok3, x1, wgu, bgu):
    nblk = sched[0].shape[0]
    rows = D_MODEL // W_CHUNKS
    smem_blk = lambda off: pl.BlockSpec(
        (1, 1, MOE_TM), lambda i, *_: (jnp.minimum(i + off, nblk - 1), 0, 0), memory_space=pltpu.SMEM)
    grid_spec = pltpu.PrefetchScalarGridSpec(
        num_scalar_prefetch=len(sched),
        grid=(nblk,),
        in_specs=[smem_blk(0), smem_blk(1), smem_blk(2),
                  pl.BlockSpec(memory_space=pl.ANY),
                  pl.BlockSpec(memory_space=pl.ANY),
                  pl.BlockSpec((1, 1, 2 * D_EXPERT), lambda i, be, *_: (be[i], 0, 0))],
        out_specs=pl.BlockSpec((MOE_TM, D_EXPERT), lambda i, *_: (i, 0)),
        scratch_shapes=[pltpu.VMEM((X_SLOTS, MOE_TM, D_MODEL), F32), pltpu.SemaphoreType.DMA((X_SLOTS,)),
                        pltpu.VMEM((2, D_MODEL, 2 * D_EXPERT), BF16),
                        pltpu.VMEM((2, rows, 2 * D_EXPERT), F32), pltpu.SemaphoreType.DMA((2,))],
    )
    return pl.pallas_call(
        _moe_up_kernel,
        grid_spec=grid_spec,
        out_shape=jax.ShapeDtypeStruct((nblk * MOE_TM, D_EXPERT), BF16),
        compiler_params=_cparams(("arbitrary",)),
        name="moe_up",
    )(*sched, row_tok3, row_tok3, row_tok3, x1, wgu, bgu)


def _moe_down_kernel(be_ref, nu_ref, par_ref, t0_ref, t1_ref, te_ref, nt_ref,
                     h_ref, w_hbm, b_ref, y_ref, wbf, stg, wsem):
    i = pl.program_id(0)
    _stream_expert_weights(i, t0_ref, t1_ref, te_ref, nt_ref, w_hbm, stg, wsem, wbf)

    @pl.when(i < nu_ref[0])
    def _():
        y_ref[...] = jnp.dot(h_ref[...], wbf[par_ref[i]], preferred_element_type=F32) + b_ref[0]

    @pl.when(i >= nu_ref[0])
    def _():
        y_ref[...] = jnp.zeros(y_ref.shape, y_ref.dtype)


def _moe_down(sched, h, wd, bd):
    nblk = sched[0].shape[0]
    rows = D_EXPERT // W_CHUNKS
    grid_spec = pltpu.PrefetchScalarGridSpec(
        num_scalar_prefetch=len(sched),
        grid=(nblk,),
        in_specs=[pl.BlockSpec((MOE_TM, D_EXPERT), lambda i, be, nu, *_: (jnp.minimum(i, nu[0] - 1), 0)),
                  pl.BlockSpec(memory_space=pl.ANY),
                  pl.BlockSpec((1, 1, D_MODEL), lambda i, be, *_: (be[i], 0, 0))],
        out_specs=pl.BlockSpec((MOE_TM, D_MODEL), lambda i, *_: (i, 0)),
        scratch_shapes=[pltpu.VMEM((2, D_EXPERT, D_MODEL), BF16),
                        pltpu.VMEM((2, rows, D_MODEL), F32), pltpu.SemaphoreType.DMA((2,))],
    )
    return pl.pallas_call(
        _moe_down_kernel,
        grid_spec=grid_spec,
        out_shape=jax.ShapeDtypeStruct((nblk * MOE_TM, D_MODEL), F32),
        compiler_params=_cparams(("arbitrary",)),
        name="moe_down",
    )(*sched, h, wd, bd)


def _combine_kernel(pos_ref, posn_ref, gate_ref, x1_ref, g2_ref, b2_ref, y_hbm, o_ref, ybuf, sem):
    i = pl.program_id(0)
    n = pl.num_programs(0)
    slot = i % 2
    nrow = TOP_K * COMB_TM

    @pl.when(i == 0)
    def _():
        _row_gather(pos_ref, nrow, y_hbm, ybuf.at[0], sem.at[0])

    @pl.when(i + 1 < n)
    def _():
        _row_gather_unrolled(posn_ref, nrow, y_hbm, ybuf.at[1 - slot], sem.at[1 - slot])

    pltpu.make_async_copy(y_hbm.at[pl.ds(0, nrow)], ybuf.at[slot], sem.at[slot]).wait()
    gate = gate_ref[...]
    y = jnp.zeros((COMB_TM, D_MODEL), F32)
    for kk in range(TOP_K):
        y = y + gate[:, kk:kk + 1] * ybuf[slot, kk * COMB_TM:(kk + 1) * COMB_TM, :]
    o_ref[...] = _layer_norm(DEEPNORM_ALPHA * x1_ref[...] + y, g2_ref[...], b2_ref[...])


def _combine(pos3, gate, x1, g2, b2, yb):
    S = x1.shape[0]
    nt = S // COMB_TM
    nrow = TOP_K * COMB_TM
    smem_blk = lambda off: pl.BlockSpec(
        (1, 1, nrow), lambda i: (jnp.minimum(i + off, nt - 1), 0, 0), memory_space=pltpu.SMEM)
    return pl.pallas_call(
        _combine_kernel,
        grid=(nt,),
        in_specs=[smem_blk(0), smem_blk(1),
                  pl.BlockSpec((COMB_TM, LANES), lambda i: (i, 0)),
                  pl.BlockSpec((COMB_TM, D_MODEL), lambda i: (i, 0)),
                  pl.BlockSpec((1, D_MODEL), lambda i: (0, 0)),
                  pl.BlockSpec((1, D_MODEL), lambda i: (0, 0)),
                  pl.BlockSpec(memory_space=pl.ANY)],
        out_specs=pl.BlockSpec((COMB_TM, D_MODEL), lambda i: (i, 0)),
        out_shape=jax.ShapeDtypeStruct((S, D_MODEL), F32),
        scratch_shapes=[pltpu.VMEM((2, nrow, D_MODEL), F32), pltpu.SemaphoreType.DMA((2,))],
        compiler_params=_cparams(("arbitrary",)),
        name="combine",
    )(pos3, pos3, gate, x1, g2, b2, yb)


def _rope_tables(pos):
    half = ROT_DIM // 2
    inv_freq = ROPE_THETA ** (-jnp.arange(0, ROT_DIM, 2, dtype=F32) / ROT_DIM)
    ang = pos.astype(F32)[:, None] * inv_freq
    cos, sin = jnp.cos(ang), jnp.sin(ang)
    S = pos.shape[0]
    one = jnp.ones((S, HEAD_DIM - ROT_DIM), F32)
    zero = jnp.zeros((S, HEAD_DIM - ROT_DIM), F32)
    z16 = jnp.zeros((S, half), F32)
    rc = jnp.concatenate([cos, cos, one], axis=1)
    ra = jnp.concatenate([-sin, z16, zero], axis=1)
    rb = jnp.concatenate([z16, sin, zero], axis=1)
    return rc, ra, rb


def _routing(idx, rank, counts, S):
    n = S * TOP_K
    nblk = n // MOE_TM + N_EXPERTS
    onehot = (idx[:, :, None] == jnp.arange(N_EXPERTS, dtype=jnp.int32)[None, None, :])
    padded = (counts + MOE_TM - 1) // MOE_TM * MOE_TM
    pends = jnp.cumsum(padded)
    pstarts = pends - padded
    dest = jnp.sum(jnp.where(onehot, pstarts[None, None, :], 0), axis=-1) + rank
    tok = jnp.broadcast_to(jnp.arange(S, dtype=jnp.int32)[:, None], (S, TOP_K))
    row_tok = jnp.zeros((nblk * MOE_TM,), jnp.int32).at[dest.reshape(-1)].set(tok.reshape(-1))
    starts = jnp.arange(nblk, dtype=jnp.int32) * MOE_TM
    block_e = jnp.minimum(jnp.sum(pends[None, :] <= starts[:, None], axis=1, dtype=jnp.int32), N_EXPERTS - 1)
    nused = (pends[-1:] // MOE_TM).astype(jnp.int32)

    e_ids = jnp.arange(N_EXPERTS, dtype=jnp.int32)
    nb_e = (padded // MOE_TM).astype(jnp.int32)
    bstart_e = (pstarts // MOE_TM).astype(jnp.int32)
    nonempty = nb_e > 0
    run_of_e = jnp.cumsum(nonempty.astype(jnp.int32)) - 1
    nruns = jnp.sum(nonempty.astype(jnp.int32))
    run_e = jnp.sum(jnp.where(nonempty[None, :] & (run_of_e[None, :] == e_ids[:, None]), e_ids[None, :], 0), axis=1)
    task_e = jnp.repeat(run_e, W_CHUNKS).astype(jnp.int32)
    ntask = (W_CHUNKS * nruns).reshape(1).astype(jnp.int32)
    blk = jnp.arange(nblk, dtype=jnp.int32)
    used = blk < nused[0]
    of_block = block_e[:, None] == e_ids[None, :]
    pick = lambda v: jnp.sum(jnp.where(of_block, v[None, :], 0), axis=1)
    r_i = pick(run_of_e)
    j_i = blk - pick(bstart_e)
    n_i = jnp.maximum(pick(nb_e), 1)
    steps = jnp.arange(1, W_CHUNKS + 1, dtype=jnp.int32)[None, :]
    share = lambda j: jnp.sum(steps * n_i[:, None] <= W_CHUNKS * j[:, None], axis=1, dtype=jnp.int32)
    t0 = W_CHUNKS * (r_i + 1) + share(j_i)
    t1 = W_CHUNKS * (r_i + 1) + share(j_i + 1)
    t0 = jnp.where(blk == 0, 0, jnp.where(used, jnp.minimum(t0, ntask[0]), ntask[0])).astype(jnp.int32)
    t1 = jnp.where(used, jnp.minimum(t1, ntask[0]), ntask[0]).astype(jnp.int32)
    par = (r_i % 2).astype(jnp.int32)
    sched = (block_e, nused, par, t0, t1, task_e, ntask)
    return dest.astype(jnp.int32), row_tok, sched


def kernel(x, positions, w_in, w_attn_out, sgu_ln_g, sgu_ln_b, sgu_w, sgu_b, w_sgu_out, w_out,
           ln1_g, ln1_b, w_router, b_router, w_gate_up, b_gate_up, w_down, b_down, ln2_g, ln2_b):
    B, S, D = x.shape
    assert B == 1 and D == D_MODEL and w_in.shape[0] == DEPTH
    assert S % PROJ_TM == 0 and S % (DIL_PATTERNS[-1][1] * ATTN_QB * ATTN_BLK) == 0
    x2d = x.reshape(S, D)

    a0, a1, a2 = ATTN_WIDTH, 2 * ATTN_WIDTH, 3 * ATTN_WIDTH
    z1 = a2 + 2 * SGU_WIDTH
    wi = w_in[0]
    qkv_cols = [wi[:, p * ATTN_WIDTH + gi * GROUP_W:p * ATTN_WIDTH + (gi + 1) * GROUP_W]
                for gi in range(N_GROUPS) for p in range(3)]
    w_cat = jnp.concatenate([wi[:, a2:z1], wi[:, z1:z1 + D], wi[:, z1 + D:]] + qkv_cols, axis=1).astype(BF16)
    rc, ra, rb = _rope_tables(positions[0])

    main, qkv2, qkv3 = _inproj(x2d, w_cat, rc, ra, rb)
    for window, d in DIL_PATTERNS:
        assert window // (2 * d) == ATTN_BLK // 2
    o1, l1 = _attention(main.reshape(1, S, main.shape[1]), QKV_BLK0, QKV_BLK0 + 1, QKV_BLK0 + 2)
    o2, l2 = _attention(qkv2, 0, 1, 2)
    o3, l3 = _attention(qkv3, 0, 1, 2)

    sbias = jnp.repeat(sgu_b[0].T, CHUNK, axis=1)
    wr = jnp.pad(w_router[0], ((0, 0), (0, LANES - N_EXPERTS))).astype(BF16)
    br = jnp.pad(b_router[0], (0, LANES - N_EXPERTS)).reshape(1, LANES)
    x1, idx_f, gate, rank_f, cnt_f = _tail(
        main, o1.reshape(S, GROUP_W), o2, o3, l1.reshape(S, LANES), l2, l3, x2d,
        sgu_w[0].astype(BF16), sbias, sgu_ln_g[0].reshape(1, -1), sgu_ln_b[0].reshape(1, -1),
        w_sgu_out[0].astype(BF16), w_attn_out[0].astype(BF16), w_out[0].astype(BF16),
        ln1_g[0].reshape(1, -1), ln1_b[0].reshape(1, -1), wr, br)

    idx = idx_f[:, :TOP_K].astype(jnp.int32)
    rank = rank_f[:, :TOP_K].astype(jnp.int32)
    counts = cnt_f[0, :N_EXPERTS].astype(jnp.int32)
    dest, row_tok, sched = _routing(idx, rank, counts, S)
    nblk = sched[0].shape[0]

    h = _moe_up(sched, row_tok.reshape(nblk, 1, MOE_TM), x1, w_gate_up[0], b_gate_up[0].reshape(N_EXPERTS, 1, -1))
    yb = _moe_down(sched, h, w_down[0], b_down[0].reshape(N_EXPERTS, 1, -1))

    nt = S // COMB_TM
    pos3 = dest.reshape(nt, COMB_TM, TOP_K).transpose(0, 2, 1).reshape(nt, 1, TOP_K * COMB_TM)
    out = _combine(pos3, gate, x1, ln2_g[0].reshape(1, -1), ln2_b[0].reshape(1, -1), yb)
    return out.reshape(B, S, D)
```

```python
import numpy as np
import jax
import jax.numpy as jnp
from jax import lax
from jax.experimental import pallas as pl
from jax.experimental.pallas import tpu as pltpu

F32 = jnp.float32
BF16 = jnp.bfloat16

D_MODEL = 2048
HEAD_DIM = 128
ROT_DIM = HEAD_DIM // 4
ROPE_THETA = 500000.0
DIL_PATTERNS = ((128, 1), (512, 4), (2048, 16))
N_GROUPS = len(DIL_PATTERNS)
HEADS = 4
GROUP_W = HEADS * HEAD_DIM
ATTN_WIDTH = N_GROUPS * GROUP_W
CHUNK = 128
SGU_GROUPS = 8
SGU_WIDTH = SGU_GROUPS * CHUNK
IN_WIDTH = 3 * ATTN_WIDTH + 2 * SGU_WIDTH + 2 * D_MODEL
N_EXPERTS = 32
TOP_K = 4
D_EXPERT = D_MODEL
SWIGLU_LIMIT = 7.0
SWIGLU_ALPHA = 1.702
LN_EPS = 1e-5
DEPTH = 1
DEEPNORM_ALPHA = (2 * DEPTH) ** 0.25
NEG_INF = -1e30

LANES = 128
V7X_VMEM_BYTES = 64 * 1024 * 1024
VMEM_LIMIT = V7X_VMEM_BYTES - 6 * 1024 * 1024

COL_BLK = 512
PROJ_TM = 1024
PROJ_CHUNKS = 3
ATTN_BLK = 128
ATTN_QB = 4
ATTN_AHEAD = 10
TAIL_TM = 256
MOE_TM = 256
W_CHUNKS = 8
X_SLOTS = 3
COMB_TM = 128

Z_BLK0, GA_BLK0, GB_BLK0, QKV_BLK0 = 0, 4, 8, 12
N_PROJ_BLKS = IN_WIDTH // COL_BLK
MAIN_TILES = 5
assert N_PROJ_BLKS == 21 and N_PROJ_BLKS == (MAIN_TILES + 2) * PROJ_CHUNKS and QKV_BLK0 == (MAIN_TILES - 1) * PROJ_CHUNKS


def _cparams(sem):
    return pltpu.CompilerParams(dimension_semantics=sem, vmem_limit_bytes=VMEM_LIMIT)


def _layer_norm(v, g, b):
    mu = jnp.mean(v, axis=-1, keepdims=True)
    c = v - mu
    var = jnp.mean(c * c, axis=-1, keepdims=True)
    return c * lax.rsqrt(var + LN_EPS) * g + b


def _epilogue_kind(blk):
    if blk < GA_BLK0:
        return "gelu"
    if blk < QKV_BLK0:
        return "sigmoid"
    return "plain" if (blk - QKV_BLK0) % 3 == 2 else "rope"


def _inproj_kernel(x_ref, w_ref, rc_ref, ra_ref, rb_ref, o_ref, g2_ref, g3_ref, xb_ref, racc_ref):
    j = pl.program_id(1)
    tm = x_ref.shape[0]

    def store(c, val, residue_major):
        cs = slice(c * COL_BLK, (c + 1) * COL_BLK)
        if residue_major is None:
            o_ref[:, cs] = val.astype(BF16)
            return
        ref, d = residue_major
        for k in range(COL_BLK // LANES):
            racc_ref[k] = val[:, k * LANES:(k + 1) * LANES]
        for r in range(d):
            parts = [racc_ref[k, pl.ds(r, tm // d, stride=d), :] for k in range(COL_BLK // LANES)]
            ref[r, :, cs] = jnp.concatenate(parts, axis=1).astype(BF16)

    @pl.when(j == 0)
    def _():
        xb_ref[...] = x_ref[...].astype(BF16)

    def epilogue(acc, kind):
        if kind == "gelu":
            return 0.5 * acc * (1.0 + lax.erf(acc * (2.0 ** -0.5)))
        if kind == "sigmoid":
            return 1.0 / (1.0 + jnp.exp(-acc))
        if kind == "rope":
            c = jnp.concatenate([rc_ref[...]] * HEADS, axis=1)
            a = jnp.concatenate([ra_ref[...]] * HEADS, axis=1)
            b = jnp.concatenate([rb_ref[...]] * HEADS, axis=1)
            half = ROT_DIM // 2
            up = pltpu.roll(acc, COL_BLK - half, axis=1)
            dn = pltpu.roll(acc, half, axis=1)
            return acc * c + up * a + dn * b
        return acc

    dests = {MAIN_TILES: (g2_ref, DIL_PATTERNS[1][1]), MAIN_TILES + 1: (g3_ref, DIL_PATTERNS[2][1])}
    patterns = {}
    for t in range(N_PROJ_BLKS // PROJ_CHUNKS):
        kinds = tuple(_epilogue_kind(t * PROJ_CHUNKS + c) for c in range(PROJ_CHUNKS))
        patterns.setdefault((kinds, t if t in dests else None), []).append(t)
    for (kinds, special), tiles in patterns.items():
        cond = j == tiles[0]
        for t in tiles[1:]:
            cond = cond | (j == t)

        @pl.when(cond)
        def _(kinds=kinds, special=special):
            for c, kind in enumerate(kinds):
                cs = slice(c * COL_BLK, (c + 1) * COL_BLK)
                acc = jnp.dot(xb_ref[...], w_ref[:, cs], preferred_element_type=F32)
                store(c, epilogue(acc, kind), dests.get(special))


def _inproj(x2d, w_cat, rc, ra, rb):
    S = x2d.shape[0]
    tm = PROJ_TM
    tn = PROJ_CHUNKS * COL_BLK
    d2, d3 = DIL_PATTERNS[1][1], DIL_PATTERNS[2][1]
    return pl.pallas_call(
        _inproj_kernel,
        grid=(S // tm, IN_WIDTH // tn),
        in_specs=[
            pl.BlockSpec((tm, D_MODEL), lambda i, j: (i, 0)),
            pl.BlockSpec((D_MODEL, tn), lambda i, j: (0, j)),
            pl.BlockSpec((tm, LANES), lambda i, j: (i, 0)),
            pl.BlockSpec((tm, LANES), lambda i, j: (i, 0)),
            pl.BlockSpec((tm, LANES), lambda i, j: (i, 0)),
        ],
        out_specs=[pl.BlockSpec((tm, tn), lambda i, j: (i, jnp.minimum(j, MAIN_TILES - 1))),
                   pl.BlockSpec((d2, tm // d2, tn), lambda i, j: (0, i, 0)),
                   pl.BlockSpec((d3, tm // d3, tn), lambda i, j: (0, i, 0))],
        out_shape=[jax.ShapeDtypeStruct((S, MAIN_TILES * tn), BF16),
                   jax.ShapeDtypeStruct((d2, S // d2, tn), BF16),
                   jax.ShapeDtypeStruct((d3, S // d3, tn), BF16)],
        scratch_shapes=[pltpu.VMEM((tm, D_MODEL), BF16),
                        pltpu.VMEM((COL_BLK // LANES, tm, LANES), F32)],
        compiler_params=_cparams(("arbitrary", "arbitrary")),
        name="inproj",
    )(x2d, w_cat, rc, ra, rb)


def _attn_kernel(q_ref, kp_ref, kc_ref, kn_ref, vp_ref, vc_ref, vn_ref, band_ref, o_ref, l_ref):
    i = pl.program_id(1)
    n = pl.num_programs(1)
    k = jnp.concatenate([kp_ref[0], kc_ref[0], kn_ref[0]], axis=0)
    v = jnp.concatenate([vp_ref[0], vc_ref[0], vn_ref[0]], axis=0)
    hq = ATTN_BLK // 2
    col = lax.broadcasted_iota(jnp.int32, (hq, 2 * ATTN_BLK), 1)
    lane = lax.broadcasted_iota(jnp.int32, (hq, LANES), 1)
    scale = HEAD_DIM ** -0.5
    first_bias = jnp.where(col >= jnp.where(i > 0, 0, ATTN_BLK), band_ref[0], NEG_INF)
    last_bias = jnp.where(col < jnp.where(i < n - 1, 2 * ATTN_BLK, ATTN_BLK), band_ref[1], NEG_INF)

    units = [(s, hf, h) for s in range(ATTN_QB) for hf in range(2) for h in range(HEADS)]

    def scores(s, hf, h):
        qs = slice(s * ATTN_BLK + hf * hq, s * ATTN_BLK + (hf + 1) * hq)
        ks = slice((s + hf) * ATTN_BLK, (s + hf + 2) * ATTN_BLK)
        hs = slice(h * HEAD_DIM, (h + 1) * HEAD_DIM)
        return lax.dot_general(q_ref[0, qs, hs], k[ks, hs], (((1,), (1,)), ((), ())),
                               preferred_element_type=F32)

    pending = [scores(*u) for u in units[:ATTN_AHEAD]]
    lse_tile = None
    for j, (s, hf, h) in enumerate(units):
        raw = pending.pop(0)
        if j + ATTN_AHEAD < len(units):
            pending.append(scores(*units[j + ATTN_AHEAD]))
        if s == 0 and hf == 0:
            bias = first_bias
        elif s == ATTN_QB - 1 and hf == 1:
            bias = last_bias
        else:
            bias = band_ref[hf]
        qs = slice(s * ATTN_BLK + hf * hq, s * ATTN_BLK + (hf + 1) * hq)
        ks = slice((s + hf) * ATTN_BLK, (s + hf + 2) * ATTN_BLK)
        hs = slice(h * HEAD_DIM, (h + 1) * HEAD_DIM)
        sc = raw * scale + bias
        m = jnp.max(sc, axis=-1, keepdims=True)
        p = jnp.exp(sc - m)
        den = jnp.sum(p, axis=-1, keepdims=True)
        o = jnp.dot(p.astype(BF16), v[ks, hs], preferred_element_type=F32) / den
        o_ref[0, qs, hs] = o.astype(BF16)
        if h == 0:
            lse_tile = jnp.zeros((hq, LANES), F32)
        lse_tile = jnp.where(lane == h, m + jnp.log(den), lse_tile)
        if h == HEADS - 1:
            l_ref[0, qs, :] = lse_tile


def _band_bias():
    hq = ATTN_BLK // 2
    out = np.empty((2, hq, 2 * ATTN_BLK), np.float32)
    for hf in range(2):
        i = np.arange(hq)[:, None] + hf * hq
        c = np.arange(2 * ATTN_BLK)[None, :] + hf * ATTN_BLK
        rel = c - ATTN_BLK - i
        out[hf] = np.where(np.abs(rel) <= hq, 0.0, NEG_INF)
    return jnp.asarray(out)


def _attention(arr, qb, kb, vb):
    d, L, _ = arr.shape
    assert ATTN_QB >= 2 and L % (ATTN_QB * ATTN_BLK) == 0
    nb = L // ATTN_BLK
    big = (1, ATTN_QB * ATTN_BLK, GROUP_W)
    small = (1, ATTN_BLK, GROUP_W)
    cur = lambda cb: pl.BlockSpec(big, lambda r, i: (r, i, cb))
    prev = lambda cb: pl.BlockSpec(small, lambda r, i: (r, jnp.maximum(ATTN_QB * i - 1, 0), cb))
    nxt = lambda cb: pl.BlockSpec(small, lambda r, i: (r, jnp.minimum(ATTN_QB * (i + 1), nb - 1), cb))
    return pl.pallas_call(
        _attn_kernel,
        grid=(d, nb // ATTN_QB),
        in_specs=[cur(qb), prev(kb), cur(kb), nxt(kb), prev(vb), cur(vb), nxt(vb),
                  pl.BlockSpec((2, ATTN_BLK // 2, 2 * ATTN_BLK), lambda r, i: (0, 0, 0))],
        out_specs=[pl.BlockSpec(big, lambda r, i: (r, i, 0)),
                   pl.BlockSpec((1, ATTN_QB * ATTN_BLK, LANES), lambda r, i: (r, i, 0))],
        out_shape=[jax.ShapeDtypeStruct((d, L, GROUP_W), BF16),
                   jax.ShapeDtypeStruct((d, L, LANES), F32)],
        compiler_params=_cparams(("arbitrary", "arbitrary")),
        name=f"attn_d{d}",
    )(arr, arr, arr, arr, arr, arr, arr, _band_bias())


def _tail_kernel(z_ref, ga_ref, gb_ref, o1_ref, o2_ref, o3_ref, l1_ref, l2_ref, l3_ref, x_ref,
                 sw_ref, sbias_ref, slg_ref, slb_ref, wso_ref, wao_ref, wo_ref,
                 g1_ref, b1_ref, wr_ref, br_ref, ltri_ref,
                 x1_ref, idx_ref, gate_ref, rank_ref, cnt_ref,
                 carry_ref, on2_ref, ln2_ref, on3_ref, ln3_ref):
    tm = z_ref.shape[0]

    @pl.when(pl.program_id(0) == 0)
    def _():
        carry_ref[...] = jnp.zeros(carry_ref.shape, carry_ref.dtype)

    z = z_ref[...]
    u = z[:, :SGU_WIDTH].astype(F32)
    vn = _layer_norm(z[:, SGU_WIDTH:].astype(F32), slg_ref[...], slb_ref[...]).astype(BF16)
    rows = []
    for c in range(tm // CHUNK):
        cols = []
        for g in range(SGU_GROUPS):
            blk = vn[c * CHUNK:(c + 1) * CHUNK, g * CHUNK:(g + 1) * CHUNK]
            cols.append(jnp.dot(sw_ref[g], blk, preferred_element_type=F32))
        rows.append(jnp.concatenate(cols, axis=1) + sbias_ref[...])
    mixed = jnp.concatenate(rows, axis=0)
    y_b = jnp.dot((u * mixed).astype(BF16), wso_ref[...], preferred_element_type=F32)

    def to_position_order(o_ref, l_ref, onat_ref, lnat_ref):
        d = o_ref.shape[0]
        for r in range(d):
            rows = pl.ds(r, tm // d, stride=d)
            for h in range(HEADS):
                onat_ref[h, rows, :] = o_ref[r, :, h * HEAD_DIM:(h + 1) * HEAD_DIM].astype(F32)
            lnat_ref[rows, :] = l_ref[r]

    to_position_order(o2_ref, l2_ref, on2_ref, ln2_ref)
    to_position_order(o3_ref, l3_ref, on3_ref, ln3_ref)

    l1, l2, l3 = l1_ref[...], ln2_ref[...], ln3_ref[...]
    m = jnp.maximum(jnp.maximum(l1, l2), l3)
    e1, e2, e3 = jnp.exp(l1 - m), jnp.exp(l2 - m), jnp.exp(l3 - m)
    inv = 1.0 / (e1 + e2 + e3)
    w1, w2, w3 = e1 * inv, e2 * inv, e3 * inv
    parts = []
    for h in range(HEADS):
        hs = slice(h * HEAD_DIM, (h + 1) * HEAD_DIM)
        parts.append(w1[:, h:h + 1] * o1_ref[:, hs].astype(F32)
                     + w2[:, h:h + 1] * on2_ref[h]
                     + w3[:, h:h + 1] * on3_ref[h])
    o_a = jnp.concatenate(parts, axis=1).astype(BF16)
    y_a = jnp.dot(o_a, wao_ref[...], preferred_element_type=F32)

    merged = ga_ref[...].astype(F32) * y_a + gb_ref[...].astype(F32) * y_b
    hmix = jnp.dot(merged.astype(BF16), wo_ref[...], preferred_element_type=F32)
    x1 = _layer_norm(DEEPNORM_ALPHA * x_ref[...] + hmix, g1_ref[...], b1_ref[...])
    x1_ref[...] = x1

    logits = jnp.dot(x1.astype(BF16), wr_ref[...], preferred_element_type=F32) + br_ref[...]
    lane = lax.broadcasted_iota(jnp.int32, (tm, LANES), 1)
    lane_f = lane.astype(F32)
    vals = jnp.where(lane < N_EXPERTS, logits, -jnp.inf)
    idx_tile = jnp.zeros((tm, LANES), F32)
    val_tile = jnp.zeros((tm, LANES), F32)
    member = jnp.zeros((tm, LANES), F32)
    top0 = None
    picks = []
    for kk in range(TOP_K):
        mk = jnp.max(vals, axis=-1, keepdims=True)
        ik = jnp.min(jnp.where(vals == mk, lane_f, float(LANES)), axis=-1, keepdims=True)
        hit = lane_f == ik
        vals = jnp.where(hit, -jnp.inf, vals)
        member = jnp.where(hit, 1.0, member)
        picks.append(hit)
        if kk == 0:
            top0 = mk
        idx_tile = jnp.where(lane == kk, ik, idx_tile)
        val_tile = jnp.where(lane == kk, jnp.exp(mk - top0), val_tile)
    idx_ref[...] = idx_tile
    gate_ref[...] = val_tile / jnp.sum(val_tile, axis=-1, keepdims=True)

    before = jnp.dot(ltri_ref[...], member.astype(BF16), preferred_element_type=F32) + carry_ref[0:1, :]
    rank_tile = jnp.zeros((tm, LANES), F32)
    for kk in range(TOP_K):
        rk = jnp.sum(jnp.where(picks[kk], before, 0.0), axis=-1, keepdims=True)
        rank_tile = jnp.where(lane == kk, rk, rank_tile)
    rank_ref[...] = rank_tile
    carry_ref[...] = carry_ref[...] + jnp.sum(member, axis=0, keepdims=True)
    cnt_ref[...] = carry_ref[...]


def _tail(proj, o1, o2, o3, l1, l2, l3, x2d, sw, sbias, slg, slb, wso, wao, wo, g1, b1, wr, br):
    S = x2d.shape[0]
    tm = TAIL_TM
    row = lambda w: pl.BlockSpec((tm, w), lambda i: (i, 0))

    def const(shape):
        nd = len(shape)
        return pl.BlockSpec(shape, lambda i: (0,) * nd, pipeline_mode=pl.Buffered(1))

    wide = lambda cb: pl.BlockSpec((tm, D_MODEL), lambda i: (i, cb))
    res = lambda a: pl.BlockSpec((a.shape[0], tm // a.shape[0], a.shape[2]), lambda i: (0, i, 0))
    r_, c_ = np.arange(tm)[:, None], np.arange(tm)[None, :]
    ltri = jnp.asarray(c_ < r_, BF16)
    return pl.pallas_call(
        _tail_kernel,
        grid=(S // tm,),
        in_specs=[wide(0), wide(1), wide(2),
                  row(GROUP_W), res(o2), res(o3), row(LANES), res(l2), res(l3),
                  row(D_MODEL),
                  const(sw.shape), const(sbias.shape), const(slg.shape), const(slb.shape),
                  const(wso.shape), const(wao.shape), const(wo.shape),
                  const(g1.shape), const(b1.shape), const(wr.shape), const(br.shape), const(ltri.shape)],
        out_specs=[row(D_MODEL), row(LANES), row(LANES), row(LANES),
                   pl.BlockSpec((8, LANES), lambda i: (0, 0))],
        out_shape=[jax.ShapeDtypeStruct((S, D_MODEL), F32),
                   jax.ShapeDtypeStruct((S, LANES), F32),
                   jax.ShapeDtypeStruct((S, LANES), F32),
                   jax.ShapeDtypeStruct((S, LANES), F32),
                   jax.ShapeDtypeStruct((8, LANES), F32)],
        scratch_shapes=[pltpu.VMEM((8, LANES), F32),
                        pltpu.VMEM((HEADS, tm, HEAD_DIM), F32), pltpu.VMEM((tm, LANES), F32),
                        pltpu.VMEM((HEADS, tm, HEAD_DIM), F32), pltpu.VMEM((tm, LANES), F32)],
        compiler_params=_cparams(("arbitrary",)),
        name="tail",
    )(proj, proj, proj, o1, o2, o3, l1, l2, l3, x2d, sw, sbias, slg, slb, wso, wao, wo, g1, b1, wr, br, ltri)


def _row_gather(idx_ref, n, src_hbm, dst, sem):
    def body(r, carry):
        t = idx_ref[0, 0, r]
        pltpu.make_async_copy(src_hbm.at[pl.ds(t, 1)], dst.at[pl.ds(r, 1)], sem).start()
        return carry
    lax.fori_loop(0, n, body, 0, unroll=8)


def _row_gather_unrolled(idx_ref, n, src_hbm, dst, sem):
    for r in range(n):
        t = idx_ref[0, 0, r]
        pltpu.make_async_copy(src_hbm.at[pl.ds(t, 1)], dst.at[pl.ds(r, 1)], sem).start()


def _stream_expert_weights(i, t0_ref, t1_ref, te_ref, nt_ref, w_hbm, stg, wsem, wbf):
    rows = stg.shape[1]
    ntask = nt_ref[0]

    def start(k):
        c = k % W_CHUNKS
        src = w_hbm.at[te_ref[k], pl.ds(pl.multiple_of(c * rows, rows), rows), :]
        pltpu.make_async_copy(src, stg.at[k % 2], wsem.at[k % 2]).start(priority=1)

    @pl.when(i == 0)
    def _():
        start(0)
        start(1)

    def body(k, carry):
        slot = k % 2
        pltpu.make_async_copy(w_hbm.at[0, pl.ds(0, rows), :], stg.at[slot], wsem.at[slot]).wait()
        par = (k // W_CHUNKS) % 2
        c = k % W_CHUNKS
        wbf[par, pl.ds(pl.multiple_of(c * rows, rows), rows), :] = stg[slot].astype(BF16)

        @pl.when(k + 2 < ntask)
        def _():
            start(k + 2)
        return carry

    lax.fori_loop(t0_ref[i], t1_ref[i], body, 0)


def _moe_up_kernel(be_ref, nu_ref, par_ref, t0_ref, t1_ref, te_ref, nt_ref,
                   tok_ref, tok1_ref, tok2_ref, x_hbm, w_hbm, b_ref, h_ref, xbuf, xsem, wbf, stg, wsem):
    i = pl.program_id(0)
    nused = nu_ref[0]
    slot = i % X_SLOTS

    @pl.when(i == 0)
    def _():
        _row_gather(tok_ref, MOE_TM, x_hbm, xbuf.at[0], xsem.at[0])

    @pl.when((i == 0) & (nused > 1))
    def _():
        _row_gather(tok1_ref, MOE_TM, x_hbm, xbuf.at[1], xsem.at[1])

    _stream_expert_weights(i, t0_ref, t1_ref, te_ref, nt_ref, w_hbm, stg, wsem, wbf)

    def compute(gather_ahead):
        if gather_ahead:
            ahead = (i + 2) % X_SLOTS
            _row_gather_unrolled(tok2_ref, MOE_TM, x_hbm, xbuf.at[ahead], xsem.at[ahead])
        pltpu.make_async_copy(x_hbm.at[pl.ds(0, MOE_TM)], xbuf.at[slot], xsem.at[slot]).wait()
        xb = xbuf[slot].astype(BF16)
        w = wbf.at[par_ref[i]]
        for c in range(D_EXPERT // COL_BLK):
            cs = slice(c * COL_BLK, (c + 1) * COL_BLK)
            us = slice(D_EXPERT + c * COL_BLK, D_EXPERT + (c + 1) * COL_BLK)
            g = jnp.dot(xb, w[:, cs], preferred_element_type=F32) + b_ref[0, :, cs]
            up = jnp.dot(xb, w[:, us], preferred_element_type=F32) + b_ref[0, :, us]
            g = jnp.minimum(g, SWIGLU_LIMIT)
            up = jnp.clip(up, -SWIGLU_LIMIT, SWIGLU_LIMIT)
            act = (up + 1.0) * (g * (1.0 / (1.0 + jnp.exp(-SWIGLU_ALPHA * g))))
            h_ref[:, cs] = act.astype(BF16)

    @pl.when(i + 2 < nused)
    def _():
        compute(True)

    @pl.when((i + 2 >= nused) & (i < nused))
    def _():
        compute(False)

    @pl.when(i >= nused)
    def _():
        h_ref[...] = jnp.zeros(h_ref.shape, h_ref.dtype)


def _moe_up(sched, row_tok3, x1, wgu, bgu):
    nblk = sched[0].shape[0]
    rows = D_MODEL // W_CHUNKS
    smem_blk = lambda off: pl.BlockSpec(
        (1, 1, MOE_TM), lambda i, *_: (jnp.minimum(i + off, nblk - 1), 0, 0), memory_space=pltpu.SMEM)
    grid_spec = pltpu.PrefetchScalarGridSpec(
        num_scalar_prefetch=len(sched),
        grid=(nblk,),
        in_specs=[smem_blk(0), smem_blk(1), smem_blk(2),
                  pl.BlockSpec(memory_space=pl.ANY),
                  pl.BlockSpec(memory_space=pl.ANY),
                  pl.BlockSpec((1, 1, 2 * D_EXPERT), lambda i, be, *_: (be[i], 0, 0))],
        out_specs=pl.BlockSpec((MOE_TM, D_EXPERT), lambda i, *_: (i, 0)),
        scratch_shapes=[pltpu.VMEM((X_SLOTS, MOE_TM, D_MODEL), F32), pltpu.SemaphoreType.DMA((X_SLOTS,)),
                        pltpu.VMEM((2, D_MODEL, 2 * D_EXPERT), BF16),
                        pltpu.VMEM((2, rows, 2 * D_EXPERT), F32), pltpu.SemaphoreType.DMA((2,))],
    )
    return pl.pallas_call(
        _moe_up_kernel,
        grid_spec=grid_spec,
        out_shape=jax.ShapeDtypeStruct((nblk * MOE_TM, D_EXPERT), BF16),
        compiler_params=_cparams(("arbitrary",)),
        name="moe_up",
    )(*sched, row_tok3, row_tok3, row_tok3, x1, wgu, bgu)


def _moe_down_kernel(be_ref, nu_ref, par_ref, t0_ref, t1_ref, te_ref, nt_ref,
                     h_ref, w_hbm, b_ref, y_ref, wbf, stg, wsem):
    i = pl.program_id(0)
    _stream_expert_weights(i, t0_ref, t1_ref, te_ref, nt_ref, w_hbm, stg, wsem, wbf)

    @pl.when(i < nu_ref[0])
    def _():
        y_ref[...] = jnp.dot(h_ref[...], wbf[par_ref[i]], preferred_element_type=F32) + b_ref[0]

    @pl.when(i >= nu_ref[0])
    def _():
        y_ref[...] = jnp.zeros(y_ref.shape, y_ref.dtype)


def _moe_down(sched, h, wd, bd):
    nblk = sched[0].shape[0]
    rows = D_EXPERT // W_CHUNKS
    grid_spec = pltpu.PrefetchScalarGridSpec(
        num_scalar_prefetch=len(sched),
        grid=(nblk,),
        in_specs=[pl.BlockSpec((MOE_TM, D_EXPERT), lambda i, be, nu, *_: (jnp.minimum(i, nu[0] - 1), 0)),
                  pl.BlockSpec(memory_space=pl.ANY),
                  pl.BlockSpec((1, 1, D_MODEL), lambda i, be, *_: (be[i], 0, 0))],
        out_specs=pl.BlockSpec((MOE_TM, D_MODEL), lambda i, *_: (i, 0)),
        scratch_shapes=[pltpu.VMEM((2, D_EXPERT, D_MODEL), BF16),
                        pltpu.VMEM((2, rows, D_MODEL), F32), pltpu.SemaphoreType.DMA((2,))],
    )
    return pl.pallas_call(
        _moe_down_kernel,
        grid_spec=grid_spec,
        out_shape=jax.ShapeDtypeStruct((nblk * MOE_TM, D_MODEL), F32),
        compiler_params=_cparams(("arbitrary",)),
        name="moe_down",
    )(*sched, h, wd, bd)


def _combine_kernel(pos_ref, posn_ref, gate_ref, x1_ref, g2_ref, b2_ref, y_hbm, o_ref, ybuf, sem):
    i = pl.program_id(0)
    n = pl.num_programs(0)
    slot = i % 2
    nrow = TOP_K * COMB_TM

    @pl.when(i == 0)
    def _():
        _row_gather(pos_ref, nrow, y_hbm, ybuf.at[0], sem.at[0])

    @pl.when(i + 1 < n)
    def _():
        _row_gather_unrolled(posn_ref, nrow, y_hbm, ybuf.at[1 - slot], sem.at[1 - slot])

    pltpu.make_async_copy(y_hbm.at[pl.ds(0, nrow)], ybuf.at[slot], sem.at[slot]).wait()
    gate = gate_ref[...]
    y = jnp.zeros((COMB_TM, D_MODEL), F32)
    for kk in range(TOP_K):
        y = y + gate[:, kk:kk + 1] * ybuf[slot, kk * COMB_TM:(kk + 1) * COMB_TM, :]
    o_ref[...] = _layer_norm(DEEPNORM_ALPHA * x1_ref[...] + y, g2_ref[...], b2_ref[...])


def _combine(pos3, gate, x1, g2, b2, yb):
    S = x1.shape[0]
    nt = S // COMB_TM
    nrow = TOP_K * COMB_TM
    smem_blk = lambda off: pl.BlockSpec(
        (1, 1, nrow), lambda i: (jnp.minimum(i + off, nt - 1), 0, 0), memory_space=pltpu.SMEM)
    return pl.pallas_call(
        _combine_kernel,
        grid=(nt,),
        in_specs=[smem_blk(0), smem_blk(1),
                  pl.BlockSpec((COMB_TM, LANES), lambda i: (i, 0)),
                  pl.BlockSpec((COMB_TM, D_MODEL), lambda i: (i, 0)),
                  pl.BlockSpec((1, D_MODEL), lambda i: (0, 0)),
                  pl.BlockSpec((1, D_MODEL), lambda i: (0, 0)),
                  pl.BlockSpec(memory_space=pl.ANY)],
        out_specs=pl.BlockSpec((COMB_TM, D_MODEL), lambda i: (i, 0)),
        out_shape=jax.ShapeDtypeStruct((S, D_MODEL), F32),
        scratch_shapes=[pltpu.VMEM((2, nrow, D_MODEL), F32), pltpu.SemaphoreType.DMA((2,))],
        compiler_params=_cparams(("arbitrary",)),
        name="combine",
    )(pos3, pos3, gate, x1, g2, b2, yb)


def _rope_tables(pos):
    half = ROT_DIM // 2
    inv_freq = ROPE_THETA ** (-jnp.arange(0, ROT_DIM, 2, dtype=F32) / ROT_DIM)
    ang = pos.astype(F32)[:, None] * inv_freq
    cos, sin = jnp.cos(ang), jnp.sin(ang)
    S = pos.shape[0]
    one = jnp.ones((S, HEAD_DIM - ROT_DIM), F32)
    zero = jnp.zeros((S, HEAD_DIM - ROT_DIM), F32)
    z16 = jnp.zeros((S, half), F32)
    rc = jnp.concatenate([cos, cos, one], axis=1)
    ra = jnp.concatenate([-sin, z16, zero], axis=1)
    rb = jnp.concatenate([z16, sin, zero], axis=1)
    return rc, ra, rb


def _routing(idx, rank, counts, S):
    n = S * TOP_K
    nblk = n // MOE_TM + N_EXPERTS
    onehot = (idx[:, :, None] == jnp.arange(N_EXPERTS, dtype=jnp.int32)[None, None, :])
    padded = (counts + MOE_TM - 1) // MOE_TM * MOE_TM
    pends = jnp.cumsum(padded)
    pstarts = pends - padded
    dest = jnp.sum(jnp.where(onehot, pstarts[None, None, :], 0), axis=-1) + rank
    tok = jnp.broadcast_to(jnp.arange(S, dtype=jnp.int32)[:, None], (S, TOP_K))
    row_tok = jnp.zeros((nblk * MOE_TM,), jnp.int32).at[dest.reshape(-1)].set(
        tok.reshape(-1), unique_indices=True, mode="promise_in_bounds")
    starts = jnp.arange(nblk, dtype=jnp.int32) * MOE_TM
    block_e = jnp.minimum(jnp.sum(pends[None, :] <= starts[:, None], axis=1, dtype=jnp.int32), N_EXPERTS - 1)
    nused = (pends[-1:] // MOE_TM).astype(jnp.int32)

    e_ids = jnp.arange(N_EXPERTS, dtype=jnp.int32)
    nb_e = (padded // MOE_TM).astype(jnp.int32)
    bstart_e = (pstarts // MOE_TM).astype(jnp.int32)
    nonempty = nb_e > 0
    run_of_e = jnp.cumsum(nonempty.astype(jnp.int32)) - 1
    nruns = jnp.sum(nonempty.astype(jnp.int32))
    run_e = jnp.sum(jnp.where(nonempty[None, :] & (run_of_e[None, :] == e_ids[:, None]), e_ids[None, :], 0), axis=1)
    task_e = jnp.repeat(run_e, W_CHUNKS).astype(jnp.int32)
    ntask = (W_CHUNKS * nruns).reshape(1).astype(jnp.int32)
    blk = jnp.arange(nblk, dtype=jnp.int32)
    used = blk < nused[0]
    of_block = block_e[:, None] == e_ids[None, :]
    pick = lambda v: jnp.sum(jnp.where(of_block, v[None, :], 0), axis=1)
    r_i = pick(run_of_e)
    j_i = blk - pick(bstart_e)
    n_i = jnp.maximum(pick(nb_e), 1)
    steps = jnp.arange(1, W_CHUNKS + 1, dtype=jnp.int32)[None, :]
    share = lambda j: jnp.sum(steps * n_i[:, None] <= W_CHUNKS * j[:, None], axis=1, dtype=jnp.int32)
    t0 = W_CHUNKS * (r_i + 1) + share(j_i)
    t1 = W_CHUNKS * (r_i + 1) + share(j_i + 1)
    t0 = jnp.where(blk == 0, 0, jnp.where(used, jnp.minimum(t0, ntask[0]), ntask[0])).astype(jnp.int32)
    t1 = jnp.where(used, jnp.minimum(t1, ntask[0]), ntask[0]).astype(jnp.int32)
    par = (r_i % 2).astype(jnp.int32)
    sched = (block_e, nused, par, t0, t1, task_e, ntask)
    return dest.astype(jnp.int32), row_tok, sched


def kernel(x, positions, w_in, w_attn_out, sgu_ln_g, sgu_ln_b, sgu_w, sgu_b, w_sgu_out, w_out,
           ln1_g, ln1_b, w_router, b_router, w_gate_up, b_gate_up, w_down, b_down, ln2_g, ln2_b):
    B, S, D = x.shape
    assert B == 1 and D == D_MODEL and w_in.shape[0] == DEPTH
    assert S % PROJ_TM == 0 and S % (DIL_PATTERNS[-1][1] * ATTN_QB * ATTN_BLK) == 0
    x2d = x.reshape(S, D)

    a0, a1, a2 = ATTN_WIDTH, 2 * ATTN_WIDTH, 3 * ATTN_WIDTH
    z1 = a2 + 2 * SGU_WIDTH
    wi = w_in[0]
    qkv_cols = [wi[:, p * ATTN_WIDTH + gi * GROUP_W:p * ATTN_WIDTH + (gi + 1) * GROUP_W]
                for gi in range(N_GROUPS) for p in range(3)]
    w_cat = jnp.concatenate([wi[:, a2:z1], wi[:, z1:z1 + D], wi[:, z1 + D:]] + qkv_cols, axis=1).astype(BF16)
    rc, ra, rb = _rope_tables(positions[0])

    main, qkv2, qkv3 = _inproj(x2d, w_cat, rc, ra, rb)
    for window, d in DIL_PATTERNS:
        assert window // (2 * d) == ATTN_BLK // 2
    o1, l1 = _attention(main.reshape(1, S, main.shape[1]), QKV_BLK0, QKV_BLK0 + 1, QKV_BLK0 + 2)
    o2, l2 = _attention(qkv2, 0, 1, 2)
    o3, l3 = _attention(qkv3, 0, 1, 2)

    sbias = jnp.repeat(sgu_b[0].T, CHUNK, axis=1)
    wr = jnp.pad(w_router[0], ((0, 0), (0, LANES - N_EXPERTS))).astype(BF16)
    br = jnp.pad(b_router[0], (0, LANES - N_EXPERTS)).reshape(1, LANES)
    x1, idx_f, gate, rank_f, cnt_f = _tail(
        main, o1.reshape(S, GROUP_W), o2, o3, l1.reshape(S, LANES), l2, l3, x2d,
        sgu_w[0].astype(BF16), sbias, sgu_ln_g[0].reshape(1, -1), sgu_ln_b[0].reshape(1, -1),
        w_sgu_out[0].astype(BF16), w_attn_out[0].astype(BF16), w_out[0].astype(BF16),
        ln1_g[0].reshape(1, -1), ln1_b[0].reshape(1, -1), wr, br)

    idx = idx_f[:, :TOP_K].astype(jnp.int32)
    rank = rank_f[:, :TOP_K].astype(jnp.int32)
    counts = cnt_f[0, :N_EXPERTS].astype(jnp.int32)
    dest, row_tok, sched = _routing(idx, rank, counts, S)
    nblk = sched[0].shape[0]

    h = _moe_up(sched, row_tok.reshape(nblk, 1, MOE_TM), x1, w_gate_up[0], b_gate_up[0].reshape(N_EXPERTS, 1, -1))
    yb = _moe_down(sched, h, w_down[0], b_down[0].reshape(N_EXPERTS, 1, -1))

    nt = S // COMB_TM
    pos3 = dest.reshape(nt, COMB_TM, TOP_K).transpose(0, 2, 1).reshape(nt, 1, TOP_K * COMB_TM)
    out = _combine(pos3, gate, x1, ln2_g[0].reshape(1, -1), ln2_b[0].reshape(1, -1), yb)
    return out.reshape(B, S, D)
```

```python
import numpy as np
import jax
import jax.numpy as jnp
from jax import lax
from jax.experimental import pallas as pl
from jax.experimental.pallas import tpu as pltpu

F32 = jnp.float32
BF16 = jnp.bfloat16

D_MODEL = 2048
HEAD_DIM = 128
ROT_DIM = HEAD_DIM // 4
ROPE_THETA = 500000.0
DIL_PATTERNS = ((128, 1), (512, 4), (2048, 16))
N_GROUPS = len(DIL_PATTERNS)
HEADS = 4
GROUP_W = HEADS * HEAD_DIM
ATTN_WIDTH = N_GROUPS * GROUP_W
CHUNK = 128
SGU_GROUPS = 8
SGU_WIDTH = SGU_GROUPS * CHUNK
IN_WIDTH = 3 * ATTN_WIDTH + 2 * SGU_WIDTH + 2 * D_MODEL
N_EXPERTS = 32
TOP_K = 4
D_EXPERT = D_MODEL
SWIGLU_LIMIT = 7.0
SWIGLU_ALPHA = 1.702
LN_EPS = 1e-5
DEPTH = 1
DEEPNORM_ALPHA = (2 * DEPTH) ** 0.25
NEG_INF = -1e30

LANES = 128
V7X_VMEM_BYTES = 64 * 1024 * 1024
VMEM_LIMIT = V7X_VMEM_BYTES - 6 * 1024 * 1024

COL_BLK = 512
PROJ_TM = 1024
PROJ_CHUNKS = 3
ATTN_BLK = 128
ATTN_QB = 8
ATTN_AHEAD = 10
TAIL_TM = 256
MOE_TM = 256
W_CHUNKS = 8
X_SLOTS = 3
COMB_TM = 256
INV_BLK = 4096

Z_BLK0, GA_BLK0, GB_BLK0, QKV_BLK0 = 0, 4, 8, 12
N_PROJ_BLKS = IN_WIDTH // COL_BLK
MAIN_TILES = 5
assert N_PROJ_BLKS == 21 and N_PROJ_BLKS == (MAIN_TILES + 2) * PROJ_CHUNKS and QKV_BLK0 == (MAIN_TILES - 1) * PROJ_CHUNKS


def _cparams(sem):
    return pltpu.CompilerParams(dimension_semantics=sem, vmem_limit_bytes=VMEM_LIMIT)


def _layer_norm(v, g, b):
    mu = jnp.mean(v, axis=-1, keepdims=True)
    c = v - mu
    var = jnp.mean(c * c, axis=-1, keepdims=True)
    return c * lax.rsqrt(var + LN_EPS) * g + b


def _epilogue_kind(blk):
    if blk < GA_BLK0:
        return "gelu"
    if blk < QKV_BLK0:
        return "sigmoid"
    return "plain" if (blk - QKV_BLK0) % 3 == 2 else "rope"


def _inproj_kernel(x_ref, w_ref, rc_ref, ra_ref, rb_ref, o_ref, g2_ref, g3_ref, xb_ref, racc_ref):
    j = pl.program_id(1)
    tm = x_ref.shape[0]

    def store(c, val, residue_major):
        cs = slice(c * COL_BLK, (c + 1) * COL_BLK)
        if residue_major is None:
            o_ref[:, cs] = val.astype(BF16)
            return
        ref, d = residue_major
        for k in range(COL_BLK // LANES):
            racc_ref[k] = val[:, k * LANES:(k + 1) * LANES]
        for r in range(d):
            parts = [racc_ref[k, pl.ds(r, tm // d, stride=d), :] for k in range(COL_BLK // LANES)]
            ref[r, :, cs] = jnp.concatenate(parts, axis=1).astype(BF16)

    @pl.when(j == 0)
    def _():
        xb_ref[...] = x_ref[...].astype(BF16)

    def epilogue(acc, kind):
        if kind == "gelu":
            return 0.5 * acc * (1.0 + lax.erf(acc * (2.0 ** -0.5)))
        if kind == "sigmoid":
            return 1.0 / (1.0 + jnp.exp(-acc))
        if kind == "rope":
            c = jnp.concatenate([rc_ref[...]] * HEADS, axis=1)
            a = jnp.concatenate([ra_ref[...]] * HEADS, axis=1)
            b = jnp.concatenate([rb_ref[...]] * HEADS, axis=1)
            half = ROT_DIM // 2
            up = pltpu.roll(acc, COL_BLK - half, axis=1)
            dn = pltpu.roll(acc, half, axis=1)
            return acc * c + up * a + dn * b
        return acc

    dests = {MAIN_TILES: (g2_ref, DIL_PATTERNS[1][1]), MAIN_TILES + 1: (g3_ref, DIL_PATTERNS[2][1])}
    patterns = {}
    for t in range(N_PROJ_BLKS // PROJ_CHUNKS):
        kinds = tuple(_epilogue_kind(t * PROJ_CHUNKS + c) for c in range(PROJ_CHUNKS))
        patterns.setdefault((kinds, t if t in dests else None), []).append(t)
    for (kinds, special), tiles in patterns.items():
        cond = j == tiles[0]
        for t in tiles[1:]:
            cond = cond | (j == t)

        @pl.when(cond)
        def _(kinds=kinds, special=special):
            for c, kind in enumerate(kinds):
                cs = slice(c * COL_BLK, (c + 1) * COL_BLK)
                acc = jnp.dot(xb_ref[...], w_ref[:, cs], preferred_element_type=F32)
                store(c, epilogue(acc, kind), dests.get(special))


def _inproj(x2d, w_cat, rc, ra, rb):
    S = x2d.shape[0]
    tm = PROJ_TM
    tn = PROJ_CHUNKS * COL_BLK
    d2, d3 = DIL_PATTERNS[1][1], DIL_PATTERNS[2][1]
    return pl.pallas_call(
        _inproj_kernel,
        grid=(S // tm, IN_WIDTH // tn),
        in_specs=[
            pl.BlockSpec((tm, D_MODEL), lambda i, j: (i, 0)),
            pl.BlockSpec((D_MODEL, tn), lambda i, j: (0, j)),
            pl.BlockSpec((tm, LANES), lambda i, j: (i, 0)),
            pl.BlockSpec((tm, LANES), lambda i, j: (i, 0)),
            pl.BlockSpec((tm, LANES), lambda i, j: (i, 0)),
        ],
        out_specs=[pl.BlockSpec((tm, tn), lambda i, j: (i, jnp.minimum(j, MAIN_TILES - 1))),
                   pl.BlockSpec((d2, tm // d2, tn), lambda i, j: (0, i, 0)),
                   pl.BlockSpec((d3, tm // d3, tn), lambda i, j: (0, i, 0))],
        out_shape=[jax.ShapeDtypeStruct((S, MAIN_TILES * tn), BF16),
                   jax.ShapeDtypeStruct((d2, S // d2, tn), BF16),
                   jax.ShapeDtypeStruct((d3, S // d3, tn), BF16)],
        scratch_shapes=[pltpu.VMEM((tm, D_MODEL), BF16),
                        pltpu.VMEM((COL_BLK // LANES, tm, LANES), F32)],
        compiler_params=_cparams(("arbitrary", "arbitrary")),
        name="inproj",
    )(x2d, w_cat, rc, ra, rb)


def _attn_kernel(q_ref, kp_ref, kc_ref, kn_ref, vp_ref, vc_ref, vn_ref, band_ref, o_ref, l_ref):
    i = pl.program_id(1)
    n = pl.num_programs(1)
    k = jnp.concatenate([kp_ref[0], kc_ref[0], kn_ref[0]], axis=0)
    v = jnp.concatenate([vp_ref[0], vc_ref[0], vn_ref[0]], axis=0)
    hq = ATTN_BLK // 2
    col = lax.broadcasted_iota(jnp.int32, (hq, 2 * ATTN_BLK), 1)
    lane = lax.broadcasted_iota(jnp.int32, (hq, LANES), 1)
    scale = HEAD_DIM ** -0.5
    first_bias = jnp.where(col >= jnp.where(i > 0, 0, ATTN_BLK), band_ref[0], NEG_INF)
    last_bias = jnp.where(col < jnp.where(i < n - 1, 2 * ATTN_BLK, ATTN_BLK), band_ref[1], NEG_INF)

    units = [(s, hf, h) for s in range(ATTN_QB) for hf in range(2) for h in range(HEADS)]

    def scores(s, hf, h):
        qs = slice(s * ATTN_BLK + hf * hq, s * ATTN_BLK + (hf + 1) * hq)
        ks = slice((s + hf) * ATTN_BLK, (s + hf + 2) * ATTN_BLK)
        hs = slice(h * HEAD_DIM, (h + 1) * HEAD_DIM)
        return lax.dot_general(q_ref[0, qs, hs], k[ks, hs], (((1,), (1,)), ((), ())),
                               preferred_element_type=F32)

    pending = [scores(*u) for u in units[:ATTN_AHEAD]]
    lse_tile = None
    for j, (s, hf, h) in enumerate(units):
        raw = pending.pop(0)
        if j + ATTN_AHEAD < len(units):
            pending.append(scores(*units[j + ATTN_AHEAD]))
        if s == 0 and hf == 0:
            bias = first_bias
        elif s == ATTN_QB - 1 and hf == 1:
            bias = last_bias
        else:
            bias = band_ref[hf]
        qs = slice(s * ATTN_BLK + hf * hq, s * ATTN_BLK + (hf + 1) * hq)
        ks = slice((s + hf) * ATTN_BLK, (s + hf + 2) * ATTN_BLK)
        hs = slice(h * HEAD_DIM, (h + 1) * HEAD_DIM)
        sc = raw * scale + bias
        m = jnp.max(sc, axis=-1, keepdims=True)
        p = jnp.exp(sc - m)
        den = jnp.sum(p, axis=-1, keepdims=True)
        o = jnp.dot(p.astype(BF16), v[ks, hs], preferred_element_type=F32) / den
        o_ref[0, qs, hs] = o.astype(BF16)
        if h == 0:
            lse_tile = jnp.zeros((hq, LANES), F32)
        lse_tile = jnp.where(lane == h, m + jnp.log(den), lse_tile)
        if h == HEADS - 1:
            l_ref[0, qs, :] = lse_tile


def _band_bias():
    hq = ATTN_BLK // 2
    out = np.empty((2, hq, 2 * ATTN_BLK), np.float32)
    for hf in range(2):
        i = np.arange(hq)[:, None] + hf * hq
        c = np.arange(2 * ATTN_BLK)[None, :] + hf * ATTN_BLK
        rel = c - ATTN_BLK - i
        out[hf] = np.where(np.abs(rel) <= hq, 0.0, NEG_INF)
    return jnp.asarray(out)


def _attention(arr, qb, kb, vb):
    d, L, _ = arr.shape
    assert ATTN_QB >= 2 and L % (ATTN_QB * ATTN_BLK) == 0
    nb = L // ATTN_BLK
    big = (1, ATTN_QB * ATTN_BLK, GROUP_W)
    small = (1, ATTN_BLK, GROUP_W)
    cur = lambda cb: pl.BlockSpec(big, lambda r, i: (r, i, cb))
    prev = lambda cb: pl.BlockSpec(small, lambda r, i: (r, jnp.maximum(ATTN_QB * i - 1, 0), cb))
    nxt = lambda cb: pl.BlockSpec(small, lambda r, i: (r, jnp.minimum(ATTN_QB * (i + 1), nb - 1), cb))
    return pl.pallas_call(
        _attn_kernel,
        grid=(d, nb // ATTN_QB),
        in_specs=[cur(qb), prev(kb), cur(kb), nxt(kb), prev(vb), cur(vb), nxt(vb),
                  pl.BlockSpec((2, ATTN_BLK // 2, 2 * ATTN_BLK), lambda r, i: (0, 0, 0))],
        out_specs=[pl.BlockSpec(big, lambda r, i: (r, i, 0)),
                   pl.BlockSpec((1, ATTN_QB * ATTN_BLK, LANES), lambda r, i: (r, i, 0))],
        out_shape=[jax.ShapeDtypeStruct((d, L, GROUP_W), BF16),
                   jax.ShapeDtypeStruct((d, L, LANES), F32)],
        compiler_params=_cparams(("arbitrary", "arbitrary")),
        name=f"attn_d{d}",
    )(arr, arr, arr, arr, arr, arr, arr, _band_bias())


def _tail_kernel(z_ref, ga_ref, gb_ref, o1_ref, o2_ref, o3_ref, l1_ref, l2_ref, l3_ref, x_ref,
                 sw_ref, sbias_ref, slg_ref, slb_ref, wso_ref, wao_ref, wo_ref,
                 g1_ref, b1_ref, wr_ref, br_ref, ltri_ref,
                 x1_ref, idx_ref, gate_ref, rank_ref, cnt_ref,
                 carry_ref, on2_ref, ln2_ref, on3_ref, ln3_ref):
    tm = z_ref.shape[0]

    @pl.when(pl.program_id(0) == 0)
    def _():
        carry_ref[...] = jnp.zeros(carry_ref.shape, carry_ref.dtype)

    z = z_ref[...]
    u = z[:, :SGU_WIDTH].astype(F32)
    vn = _layer_norm(z[:, SGU_WIDTH:].astype(F32), slg_ref[...], slb_ref[...]).astype(BF16)
    rows = []
    for c in range(tm // CHUNK):
        cols = []
        for g in range(SGU_GROUPS):
            blk = vn[c * CHUNK:(c + 1) * CHUNK, g * CHUNK:(g + 1) * CHUNK]
            cols.append(jnp.dot(sw_ref[g], blk, preferred_element_type=F32))
        rows.append(jnp.concatenate(cols, axis=1) + sbias_ref[...])
    mixed = jnp.concatenate(rows, axis=0)
    y_b = jnp.dot((u * mixed).astype(BF16), wso_ref[...], preferred_element_type=F32)

    def to_position_order(o_ref, l_ref, onat_ref, lnat_ref):
        d = o_ref.shape[0]
        for r in range(d):
            rows = pl.ds(r, tm // d, stride=d)
            for h in range(HEADS):
                onat_ref[h, rows, :] = o_ref[r, :, h * HEAD_DIM:(h + 1) * HEAD_DIM].astype(F32)
            lnat_ref[rows, :] = l_ref[r]

    to_position_order(o2_ref, l2_ref, on2_ref, ln2_ref)
    to_position_order(o3_ref, l3_ref, on3_ref, ln3_ref)

    l1, l2, l3 = l1_ref[...], ln2_ref[...], ln3_ref[...]
    m = jnp.maximum(jnp.maximum(l1, l2), l3)
    e1, e2, e3 = jnp.exp(l1 - m), jnp.exp(l2 - m), jnp.exp(l3 - m)
    inv = 1.0 / (e1 + e2 + e3)
    w1, w2, w3 = e1 * inv, e2 * inv, e3 * inv
    parts = []
    for h in range(HEADS):
        hs = slice(h * HEAD_DIM, (h + 1) * HEAD_DIM)
        parts.append(w1[:, h:h + 1] * o1_ref[:, hs].astype(F32)
                     + w2[:, h:h + 1] * on2_ref[h]
                     + w3[:, h:h + 1] * on3_ref[h])
    o_a = jnp.concatenate(parts, axis=1).astype(BF16)
    y_a = jnp.dot(o_a, wao_ref[...], preferred_element_type=F32)

    merged = ga_ref[...].astype(F32) * y_a + gb_ref[...].astype(F32) * y_b
    hmix = jnp.dot(merged.astype(BF16), wo_ref[...], preferred_element_type=F32)
    x1 = _layer_norm(DEEPNORM_ALPHA * x_ref[...] + hmix, g1_ref[...], b1_ref[...])
    x1_ref[...] = x1

    logits = jnp.dot(x1.astype(BF16), wr_ref[...], preferred_element_type=F32) + br_ref[...]
    lane = lax.broadcasted_iota(jnp.int32, (tm, LANES), 1)
    lane_f = lane.astype(F32)
    vals = jnp.where(lane < N_EXPERTS, logits, -jnp.inf)
    idx_tile = jnp.zeros((tm, LANES), F32)
    val_tile = jnp.zeros((tm, LANES), F32)
    member = jnp.zeros((tm, LANES), F32)
    top0 = None
    picks = []
    for kk in range(TOP_K):
        mk = jnp.max(vals, axis=-1, keepdims=True)
        ik = jnp.min(jnp.where(vals == mk, lane_f, float(LANES)), axis=-1, keepdims=True)
        hit = lane_f == ik
        vals = jnp.where(hit, -jnp.inf, vals)
        member = jnp.where(hit, 1.0, member)
        picks.append(hit)
        if kk == 0:
            top0 = mk
        idx_tile = jnp.where(lane == kk, ik, idx_tile)
        val_tile = jnp.where(lane == kk, jnp.exp(mk - top0), val_tile)
    idx_ref[...] = idx_tile
    gate_ref[...] = val_tile / jnp.sum(val_tile, axis=-1, keepdims=True)

    before = jnp.dot(ltri_ref[...], member.astype(BF16), preferred_element_type=F32) + carry_ref[0:1, :]
    rank_tile = jnp.zeros((tm, LANES), F32)
    for kk in range(TOP_K):
        rk = jnp.sum(jnp.where(picks[kk], before, 0.0), axis=-1, keepdims=True)
        rank_tile = jnp.where(lane == kk, rk, rank_tile)
    rank_ref[...] = rank_tile
    carry_ref[...] = carry_ref[...] + jnp.sum(member, axis=0, keepdims=True)
    cnt_ref[...] = carry_ref[...]


def _tail(proj, o1, o2, o3, l1, l2, l3, x2d, sw, sbias, slg, slb, wso, wao, wo, g1, b1, wr, br):
    S = x2d.shape[0]
    tm = TAIL_TM
    row = lambda w: pl.BlockSpec((tm, w), lambda i: (i, 0))

    def const(shape):
        nd = len(shape)
        return pl.BlockSpec(shape, lambda i: (0,) * nd, pipeline_mode=pl.Buffered(1))

    wide = lambda cb: pl.BlockSpec((tm, D_MODEL), lambda i: (i, cb))
    res = lambda a: pl.BlockSpec((a.shape[0], tm // a.shape[0], a.shape[2]), lambda i: (0, i, 0))
    r_, c_ = np.arange(tm)[:, None], np.arange(tm)[None, :]
    ltri = jnp.asarray(c_ < r_, BF16)
    return pl.pallas_call(
        _tail_kernel,
        grid=(S // tm,),
        in_specs=[wide(0), wide(1), wide(2),
                  row(GROUP_W), res(o2), res(o3), row(LANES), res(l2), res(l3),
                  row(D_MODEL),
                  const(sw.shape), const(sbias.shape), const(slg.shape), const(slb.shape),
                  const(wso.shape), const(wao.shape), const(wo.shape),
                  const(g1.shape), const(b1.shape), const(wr.shape), const(br.shape), const(ltri.shape)],
        out_specs=[row(D_MODEL), row(LANES), row(LANES), row(LANES),
                   pl.BlockSpec((8, LANES), lambda i: (0, 0))],
        out_shape=[jax.ShapeDtypeStruct((S, D_MODEL), F32),
                   jax.ShapeDtypeStruct((S, LANES), F32),
                   jax.ShapeDtypeStruct((S, LANES), F32),
                   jax.ShapeDtypeStruct((S, LANES), F32),
                   jax.ShapeDtypeStruct((8, LANES), F32)],
        scratch_shapes=[pltpu.VMEM((8, LANES), F32),
                        pltpu.VMEM((HEADS, tm, HEAD_DIM), F32), pltpu.VMEM((tm, LANES), F32),
                        pltpu.VMEM((HEADS, tm, HEAD_DIM), F32), pltpu.VMEM((tm, LANES), F32)],
        compiler_params=_cparams(("arbitrary",)),
        name="tail",
    )(proj, proj, proj, o1, o2, o3, l1, l2, l3, x2d, sw, sbias, slg, slb, wso, wao, wo, g1, b1, wr, br, ltri)


def _row_gather(idx_ref, n, src_hbm, dst, sem):
    def body(r, carry):
        t = idx_ref[0, 0, r]
        pltpu.make_async_copy(src_hbm.at[pl.ds(t, 1)], dst.at[pl.ds(r, 1)], sem).start()
        return carry
    lax.fori_loop(0, n, body, 0, unroll=8)


def _row_gather_unrolled(idx_ref, n, src_hbm, dst, sem):
    for r in range(n):
        t = idx_ref[0, 0, r]
        pltpu.make_async_copy(src_hbm.at[pl.ds(t, 1)], dst.at[pl.ds(r, 1)], sem).start()


def _stream_expert_weights(i, t0_ref, t1_ref, te_ref, nt_ref, w_hbm, stg, wsem, wbf):
    rows = stg.shape[1]
    ntask = nt_ref[0]

    def start(k):
        c = k % W_CHUNKS
        src = w_hbm.at[te_ref[k], pl.ds(pl.multiple_of(c * rows, rows), rows), :]
        pltpu.make_async_copy(src, stg.at[k % 2], wsem.at[k % 2]).start(priority=1)

    @pl.when(i == 0)
    def _():
        start(0)
        start(1)

    def body(k, carry):
        slot = k % 2
        pltpu.make_async_copy(w_hbm.at[0, pl.ds(0, rows), :], stg.at[slot], wsem.at[slot]).wait()
        par = (k // W_CHUNKS) % 2
        c = k % W_CHUNKS
        wbf[par, pl.ds(pl.multiple_of(c * rows, rows), rows), :] = stg[slot].astype(BF16)

        @pl.when(k + 2 < ntask)
        def _():
            start(k + 2)
        return carry

    lax.fori_loop(t0_ref[i], t1_ref[i], body, 0)


def _moe_up_kernel(be_ref, nu_ref, par_ref, t0_ref, t1_ref, te_ref, nt_ref,
                   tok_ref, tok1_ref, tok2_ref, x_hbm, w_hbm, b_ref, h_ref, xbuf, xsem, wbf, stg, wsem):
    i = pl.program_id(0)
    nused = nu_ref[0]
    slot = i % X_SLOTS

    @pl.when(i == 0)
    def _():
        _row_gather(tok_ref, MOE_TM, x_hbm, xbuf.at[0], xsem.at[0])

    @pl.when((i == 0) & (nused > 1))
    def _():
        _row_gather(tok1_ref, MOE_TM, x_hbm, xbuf.at[1], xsem.at[1])

    _stream_expert_weights(i, t0_ref, t1_ref, te_ref, nt_ref, w_hbm, stg, wsem, wbf)

    def compute(gather_ahead):
        if gather_ahead:
            ahead = (i + 2) % X_SLOTS
            _row_gather_unrolled(tok2_ref, MOE_TM, x_hbm, xbuf.at[ahead], xsem.at[ahead])
        pltpu.make_async_copy(x_hbm.at[pl.ds(0, MOE_TM)], xbuf.at[slot], xsem.at[slot]).wait()
        xb = xbuf[slot].astype(BF16)
        w = wbf.at[par_ref[i]]
        for c in range(D_EXPERT // COL_BLK):
            cs = slice(c * COL_BLK, (c + 1) * COL_BLK)
            us = slice(D_EXPERT + c * COL_BLK, D_EXPERT + (c + 1) * COL_BLK)
            g = jnp.dot(xb, w[:, cs], preferred_element_type=F32) + b_ref[0, :, cs]
            up = jnp.dot(xb, w[:, us], preferred_element_type=F32) + b_ref[0, :, us]
            g = jnp.minimum(g, SWIGLU_LIMIT)
            up = jnp.clip(up, -SWIGLU_LIMIT, SWIGLU_LIMIT)
            act = (up + 1.0) * (g * (1.0 / (1.0 + jnp.exp(-SWIGLU_ALPHA * g))))
            h_ref[:, cs] = act.astype(BF16)

    @pl.when(i + 2 < nused)
    def _():
        compute(True)

    @pl.when((i + 2 >= nused) & (i < nused))
    def _():
        compute(False)

    @pl.when(i >= nused)
    def _():
        h_ref[...] = jnp.zeros(h_ref.shape, h_ref.dtype)


def _moe_up(sched, row_tok3, x1, wgu, bgu):
    nblk = sched[0].shape[0]
    rows = D_MODEL // W_CHUNKS
    smem_blk = lambda off: pl.BlockSpec(
        (1, 1, MOE_TM), lambda i, *_: (jnp.minimum(i + off, nblk - 1), 0, 0), memory_space=pltpu.SMEM)
    grid_spec = pltpu.PrefetchScalarGridSpec(
        num_scalar_prefetch=len(sched),
        grid=(nblk,),
        in_specs=[smem_blk(0), smem_blk(1), smem_blk(2),
                  pl.BlockSpec(memory_space=pl.ANY),
                  pl.BlockSpec(memory_space=pl.ANY),
                  pl.BlockSpec((1, 1, 2 * D_EXPERT), lambda i, be, *_: (be[i], 0, 0))],
        out_specs=pl.BlockSpec((MOE_TM, D_EXPERT), lambda i, *_: (i, 0)),
        scratch_shapes=[pltpu.VMEM((X_SLOTS, MOE_TM, D_MODEL), F32), pltpu.SemaphoreType.DMA((X_SLOTS,)),
                        pltpu.VMEM((2, D_MODEL, 2 * D_EXPERT), BF16),
                        pltpu.VMEM((2, rows, 2 * D_EXPERT), F32), pltpu.SemaphoreType.DMA((2,))],
    )
    return pl.pallas_call(
        _moe_up_kernel,
        grid_spec=grid_spec,
        out_shape=jax.ShapeDtypeStruct((nblk * MOE_TM, D_EXPERT), BF16),
        compiler_params=_cparams(("arbitrary",)),
        name="moe_up",
    )(*sched, row_tok3, row_tok3, row_tok3, x1, wgu, bgu)


def _moe_down_kernel(be_ref, nu_ref, par_ref, t0_ref, t1_ref, te_ref, nt_ref,
                     h_ref, w_hbm, b_ref, y_ref, wbf, stg, wsem):
    i = pl.program_id(0)
    _stream_expert_weights(i, t0_ref, t1_ref, te_ref, nt_ref, w_hbm, stg, wsem, wbf)

    @pl.when(i < nu_ref[0])
    def _():
        y_ref[...] = jnp.dot(h_ref[...], wbf[par_ref[i]], preferred_element_type=F32) + b_ref[0]

    @pl.when(i >= nu_ref[0])
    def _():
        y_ref[...] = jnp.zeros(y_ref.shape, y_ref.dtype)


def _moe_down(sched, h, wd, bd):
    nblk = sched[0].shape[0]
    rows = D_EXPERT // W_CHUNKS
    grid_spec = pltpu.PrefetchScalarGridSpec(
        num_scalar_prefetch=len(sched),
        grid=(nblk,),
        in_specs=[pl.BlockSpec((MOE_TM, D_EXPERT), lambda i, be, nu, *_: (jnp.minimum(i, nu[0] - 1), 0)),
                  pl.BlockSpec(memory_space=pl.ANY),
                  pl.BlockSpec((1, 1, D_MODEL), lambda i, be, *_: (be[i], 0, 0))],
        out_specs=pl.BlockSpec((MOE_TM, D_MODEL), lambda i, *_: (i, 0)),
        scratch_shapes=[pltpu.VMEM((2, D_EXPERT, D_MODEL), BF16),
                        pltpu.VMEM((2, rows, D_MODEL), F32), pltpu.SemaphoreType.DMA((2,))],
    )
    return pl.pallas_call(
        _moe_down_kernel,
        grid_spec=grid_spec,
        out_shape=jax.ShapeDtypeStruct((nblk * MOE_TM, D_MODEL), F32),
        compiler_params=_cparams(("arbitrary",)),
        name="moe_down",
    )(*sched, h, wd, bd)


def _combine_kernel(pos_ref, posn_ref, gate_ref, x1_ref, g2_ref, b2_ref, y_hbm, o_ref, ybuf, sem):
    i = pl.program_id(0)
    n = pl.num_programs(0)
    slot = i % 2
    nrow = TOP_K * COMB_TM

    @pl.when(i == 0)
    def _():
        _row_gather(pos_ref, nrow, y_hbm, ybuf.at[0], sem.at[0])

    @pl.when(i + 1 < n)
    def _():
        _row_gather_unrolled(posn_ref, nrow, y_hbm, ybuf.at[1 - slot], sem.at[1 - slot])

    pltpu.make_async_copy(y_hbm.at[pl.ds(0, nrow)], ybuf.at[slot], sem.at[slot]).wait()
    gate = gate_ref[...]
    y = jnp.zeros((COMB_TM, D_MODEL), F32)
    for kk in range(TOP_K):
        y = y + gate[:, kk:kk + 1] * ybuf[slot, kk * COMB_TM:(kk + 1) * COMB_TM, :]
    o_ref[...] = _layer_norm(DEEPNORM_ALPHA * x1_ref[...] + y, g2_ref[...], b2_ref[...])


def _combine(pos3, gate, x1, g2, b2, yb):
    S = x1.shape[0]
    nt = S // COMB_TM
    nrow = TOP_K * COMB_TM
    smem_blk = lambda off: pl.BlockSpec(
        (1, 1, nrow), lambda i: (jnp.minimum(i + off, nt - 1), 0, 0), memory_space=pltpu.SMEM)
    return pl.pallas_call(
        _combine_kernel,
        grid=(nt,),
        in_specs=[smem_blk(0), smem_blk(1),
                  pl.BlockSpec((COMB_TM, LANES), lambda i: (i, 0)),
                  pl.BlockSpec((COMB_TM, D_MODEL), lambda i: (i, 0)),
                  pl.BlockSpec((1, D_MODEL), lambda i: (0, 0)),
                  pl.BlockSpec((1, D_MODEL), lambda i: (0, 0)),
                  pl.BlockSpec(memory_space=pl.ANY)],
        out_specs=pl.BlockSpec((COMB_TM, D_MODEL), lambda i: (i, 0)),
        out_shape=jax.ShapeDtypeStruct((S, D_MODEL), F32),
        scratch_shapes=[pltpu.VMEM((2, nrow, D_MODEL), F32), pltpu.SemaphoreType.DMA((2,))],
        compiler_params=_cparams(("arbitrary",)),
        name="combine",
    )(pos3, pos3, gate, x1, g2, b2, yb)


def _rope_tables(pos):
    half = ROT_DIM // 2
    inv_freq = ROPE_THETA ** (-jnp.arange(0, ROT_DIM, 2, dtype=F32) / ROT_DIM)
    ang = pos.astype(F32)[:, None] * inv_freq
    cos, sin = jnp.cos(ang), jnp.sin(ang)
    S = pos.shape[0]
    one = jnp.ones((S, HEAD_DIM - ROT_DIM), F32)
    zero = jnp.zeros((S, HEAD_DIM - ROT_DIM), F32)
    z16 = jnp.zeros((S, half), F32)
    rc = jnp.concatenate([cos, cos, one], axis=1)
    ra = jnp.concatenate([-sin, z16, zero], axis=1)
    rb = jnp.concatenate([z16, sin, zero], axis=1)
    return rc, ra, rb


def _invert_kernel(dest_ref, pad_lo_ref, pad_hi_ref, rt_ref):
    i = pl.program_id(0)

    @pl.when(i == 0)
    def _():
        for e in range(pad_lo_ref.shape[0]):
            def fill(p, carry):
                rt_ref[p] = 0
                return carry
            lax.fori_loop(pad_lo_ref[e], pad_hi_ref[e], fill, 0)

    base = i * INV_BLK

    def body(j, carry):
        rt_ref[dest_ref[0, 0, j]] = lax.shift_right_logical(base + j, TOP_K.bit_length() - 1)
        return carry
    lax.fori_loop(0, INV_BLK, body, 0, unroll=8)


def _invert_rows(dest, pad_lo, pad_hi, nrows):
    n = dest.size
    return pl.pallas_call(
        _invert_kernel,
        grid=(n // INV_BLK,),
        in_specs=[pl.BlockSpec((1, 1, INV_BLK), lambda i: (i, 0, 0), memory_space=pltpu.SMEM),
                  pl.BlockSpec(memory_space=pltpu.SMEM),
                  pl.BlockSpec(memory_space=pltpu.SMEM)],
        out_specs=pl.BlockSpec(memory_space=pltpu.SMEM),
        out_shape=jax.ShapeDtypeStruct((nrows,), jnp.int32),
        compiler_params=_cparams(("arbitrary",)),
        name="invert_rows",
    )(dest.reshape(n // INV_BLK, 1, INV_BLK), pad_lo, pad_hi)


def _routing(idx, rank, counts, S):
    n = S * TOP_K
    nblk = n // MOE_TM + N_EXPERTS
    onehot = (idx[:, :, None] == jnp.arange(N_EXPERTS, dtype=jnp.int32)[None, None, :])
    padded = (counts + MOE_TM - 1) // MOE_TM * MOE_TM
    pends = jnp.cumsum(padded)
    pstarts = pends - padded
    dest = jnp.sum(jnp.where(onehot, pstarts[None, None, :], 0), axis=-1) + rank
    nrows = nblk * MOE_TM
    pad_lo = jnp.concatenate([pstarts + counts, pends[-1:]]).astype(jnp.int32)
    pad_hi = jnp.concatenate([pends, jnp.full((1,), nrows, pends.dtype)]).astype(jnp.int32)
    row_tok = _invert_rows(dest.astype(jnp.int32), pad_lo, pad_hi, nrows)
    starts = jnp.arange(nblk, dtype=jnp.int32) * MOE_TM
    block_e = jnp.minimum(jnp.sum(pends[None, :] <= starts[:, None], axis=1, dtype=jnp.int32), N_EXPERTS - 1)
    nused = (pends[-1:] // MOE_TM).astype(jnp.int32)

    e_ids = jnp.arange(N_EXPERTS, dtype=jnp.int32)
    nb_e = (padded // MOE_TM).astype(jnp.int32)
    bstart_e = (pstarts // MOE_TM).astype(jnp.int32)
    nonempty = nb_e > 0
    run_of_e = jnp.cumsum(nonempty.astype(jnp.int32)) - 1
    nruns = jnp.sum(nonempty.astype(jnp.int32))
    run_e = jnp.sum(jnp.where(nonempty[None, :] & (run_of_e[None, :] == e_ids[:, None]), e_ids[None, :], 0), axis=1)
    task_e = jnp.repeat(run_e, W_CHUNKS).astype(jnp.int32)
    ntask = (W_CHUNKS * nruns).reshape(1).astype(jnp.int32)
    blk = jnp.arange(nblk, dtype=jnp.int32)
    used = blk < nused[0]
    of_block = block_e[:, None] == e_ids[None, :]
    pick = lambda v: jnp.sum(jnp.where(of_block, v[None, :], 0), axis=1)
    r_i = pick(run_of_e)
    j_i = blk - pick(bstart_e)
    n_i = jnp.maximum(pick(nb_e), 1)
    steps = jnp.arange(1, W_CHUNKS + 1, dtype=jnp.int32)[None, :]
    share = lambda j: jnp.sum(steps * n_i[:, None] <= W_CHUNKS * j[:, None], axis=1, dtype=jnp.int32)
    t0 = W_CHUNKS * (r_i + 1) + share(j_i)
    t1 = W_CHUNKS * (r_i + 1) + share(j_i + 1)
    t0 = jnp.where(blk == 0, 0, jnp.where(used, jnp.minimum(t0, ntask[0]), ntask[0])).astype(jnp.int32)
    t1 = jnp.where(used, jnp.minimum(t1, ntask[0]), ntask[0]).astype(jnp.int32)
    par = (r_i % 2).astype(jnp.int32)
    sched = (block_e, nused, par, t0, t1, task_e, ntask)
    return dest.astype(jnp.int32), row_tok, sched


def kernel(x, positions, w_in, w_attn_out, sgu_ln_g, sgu_ln_b, sgu_w, sgu_b, w_sgu_out, w_out,
           ln1_g, ln1_b, w_router, b_router, w_gate_up, b_gate_up, w_down, b_down, ln2_g, ln2_b):
    B, S, D = x.shape
    assert B == 1 and D == D_MODEL and w_in.shape[0] == DEPTH
    assert S % PROJ_TM == 0 and S % (DIL_PATTERNS[-1][1] * ATTN_QB * ATTN_BLK) == 0
    x2d = x.reshape(S, D)

    a0, a1, a2 = ATTN_WIDTH, 2 * ATTN_WIDTH, 3 * ATTN_WIDTH
    z1 = a2 + 2 * SGU_WIDTH
    wi = w_in[0]
    qkv_cols = [wi[:, p * ATTN_WIDTH + gi * GROUP_W:p * ATTN_WIDTH + (gi + 1) * GROUP_W]
                for gi in range(N_GROUPS) for p in range(3)]
    w_cat = jnp.concatenate([wi[:, a2:z1], wi[:, z1:z1 + D], wi[:, z1 + D:]] + qkv_cols, axis=1).astype(BF16)
    rc, ra, rb = _rope_tables(positions[0])

    main, qkv2, qkv3 = _inproj(x2d, w_cat, rc, ra, rb)
    for window, d in DIL_PATTERNS:
        assert window // (2 * d) == ATTN_BLK // 2
    o1, l1 = _attention(main.reshape(1, S, main.shape[1]), QKV_BLK0, QKV_BLK0 + 1, QKV_BLK0 + 2)
    o2, l2 = _attention(qkv2, 0, 1, 2)
    o3, l3 = _attention(qkv3, 0, 1, 2)

    sbias = jnp.repeat(sgu_b[0].T, CHUNK, axis=1)
    wr = jnp.pad(w_router[0], ((0, 0), (0, LANES - N_EXPERTS))).astype(BF16)
    br = jnp.pad(b_router[0], (0, LANES - N_EXPERTS)).reshape(1, LANES)
    x1, idx_f, gate, rank_f, cnt_f = _tail(
        main, o1.reshape(S, GROUP_W), o2, o3, l1.reshape(S, LANES), l2, l3, x2d,
        sgu_w[0].astype(BF16), sbias, sgu_ln_g[0].reshape(1, -1), sgu_ln_b[0].reshape(1, -1),
        w_sgu_out[0].astype(BF16), w_attn_out[0].astype(BF16), w_out[0].astype(BF16),
        ln1_g[0].reshape(1, -1), ln1_b[0].reshape(1, -1), wr, br)

    idx = idx_f[:, :TOP_K].astype(jnp.int32)
    rank = rank_f[:, :TOP_K].astype(jnp.int32)
    counts = cnt_f[0, :N_EXPERTS].astype(jnp.int32)
    dest, row_tok, sched = _routing(idx, rank, counts, S)
    nblk = sched[0].shape[0]

    h = _moe_up(sched, row_tok.reshape(nblk, 1, MOE_TM), x1, w_gate_up[0], b_gate_up[0].reshape(N_EXPERTS, 1, -1))
    yb = _moe_down(sched, h, w_down[0], b_down[0].reshape(N_EXPERTS, 1, -1))

    nt = S // COMB_TM
    pos3 = dest.reshape(nt, COMB_TM, TOP_K).transpose(0, 2, 1).reshape(nt, 1, TOP_K * COMB_TM)
    out = _combine(pos3, gate, x1, ln2_g[0].reshape(1, -1), ln2_b[0].reshape(1, -1), yb)
    return out.reshape(B, S, D)
```

```python
import numpy as np
import jax
import jax.numpy as jnp
from jax import lax
from jax.experimental import pallas as pl
from jax.experimental.pallas import tpu as pltpu

F32 = jnp.float32
BF16 = jnp.bfloat16

D_MODEL = 2048
HEAD_DIM = 128
ROT_DIM = HEAD_DIM // 4
ROPE_THETA = 500000.0
DIL_PATTERNS = ((128, 1), (512, 4), (2048, 16))
N_GROUPS = len(DIL_PATTERNS)
HEADS = 4
GROUP_W = HEADS * HEAD_DIM
ATTN_WIDTH = N_GROUPS * GROUP_W
CHUNK = 128
SGU_GROUPS = 8
SGU_WIDTH = SGU_GROUPS * CHUNK
IN_WIDTH = 3 * ATTN_WIDTH + 2 * SGU_WIDTH + 2 * D_MODEL
N_EXPERTS = 32
TOP_K = 4
D_EXPERT = D_MODEL
SWIGLU_LIMIT = 7.0
SWIGLU_ALPHA = 1.702
LN_EPS = 1e-5
DEPTH = 1
DEEPNORM_ALPHA = (2 * DEPTH) ** 0.25
NEG_INF = -1e30

LANES = 128
V7X_VMEM_BYTES = 64 * 1024 * 1024
VMEM_LIMIT = V7X_VMEM_BYTES - 6 * 1024 * 1024

COL_BLK = 512
PROJ_TM = 1024
PROJ_CHUNKS = 3
ATTN_BLK = 128
ATTN_QB = 8
ATTN_AHEAD = 10
TAIL_TM = 256
MOE_TM = 256
W_CHUNKS = 8
X_SLOTS = 3
COMB_TM = 256
INV_BLK = 4096
IDX_BATCH = 16

Z_BLK0, GA_BLK0, GB_BLK0, QKV_BLK0 = 0, 4, 8, 12
N_PROJ_BLKS = IN_WIDTH // COL_BLK
MAIN_TILES = 5
assert N_PROJ_BLKS == 21 and N_PROJ_BLKS == (MAIN_TILES + 2) * PROJ_CHUNKS and QKV_BLK0 == (MAIN_TILES - 1) * PROJ_CHUNKS


def _cparams(sem):
    return pltpu.CompilerParams(dimension_semantics=sem, vmem_limit_bytes=VMEM_LIMIT)


def _layer_norm(v, g, b):
    mu = jnp.mean(v, axis=-1, keepdims=True)
    c = v - mu
    var = jnp.mean(c * c, axis=-1, keepdims=True)
    return c * lax.rsqrt(var + LN_EPS) * g + b


def _epilogue_kind(blk):
    if blk < GA_BLK0:
        return "gelu"
    if blk < QKV_BLK0:
        return "sigmoid"
    return "plain" if (blk - QKV_BLK0) % 3 == 2 else "rope"


def _inproj_kernel(x_ref, w_ref, rc_ref, ra_ref, rb_ref, o_ref, g2_ref, g3_ref, xb_ref, racc_ref):
    j = pl.program_id(1)
    tm = x_ref.shape[0]

    def store(c, val, residue_major):
        cs = slice(c * COL_BLK, (c + 1) * COL_BLK)
        if residue_major is None:
            o_ref[:, cs] = val.astype(BF16)
            return
        ref, d = residue_major
        for k in range(COL_BLK // LANES):
            racc_ref[k] = val[:, k * LANES:(k + 1) * LANES]
        for r in range(d):
            parts = [racc_ref[k, pl.ds(r, tm // d, stride=d), :] for k in range(COL_BLK // LANES)]
            ref[r, :, cs] = jnp.concatenate(parts, axis=1).astype(BF16)

    @pl.when(j == 0)
    def _():
        xb_ref[...] = x_ref[...].astype(BF16)

    def epilogue(acc, kind):
        if kind == "gelu":
            return 0.5 * acc * (1.0 + lax.erf(acc * (2.0 ** -0.5)))
        if kind == "sigmoid":
            return 1.0 / (1.0 + jnp.exp(-acc))
        if kind == "rope":
            c = jnp.concatenate([rc_ref[...]] * HEADS, axis=1)
            a = jnp.concatenate([ra_ref[...]] * HEADS, axis=1)
            b = jnp.concatenate([rb_ref[...]] * HEADS, axis=1)
            half = ROT_DIM // 2
            up = pltpu.roll(acc, COL_BLK - half, axis=1)
            dn = pltpu.roll(acc, half, axis=1)
            return acc * c + up * a + dn * b
        return acc

    dests = {MAIN_TILES: (g2_ref, DIL_PATTERNS[1][1]), MAIN_TILES + 1: (g3_ref, DIL_PATTERNS[2][1])}
    patterns = {}
    for t in range(N_PROJ_BLKS // PROJ_CHUNKS):
        kinds = tuple(_epilogue_kind(t * PROJ_CHUNKS + c) for c in range(PROJ_CHUNKS))
        patterns.setdefault((kinds, t if t in dests else None), []).append(t)
    for (kinds, special), tiles in patterns.items():
        cond = j == tiles[0]
        for t in tiles[1:]:
            cond = cond | (j == t)

        @pl.when(cond)
        def _(kinds=kinds, special=special):
            for c, kind in enumerate(kinds):
                cs = slice(c * COL_BLK, (c + 1) * COL_BLK)
                acc = jnp.dot(xb_ref[...], w_ref[:, cs], preferred_element_type=F32)
                store(c, epilogue(acc, kind), dests.get(special))


def _inproj(x2d, w_cat, rc, ra, rb):
    S = x2d.shape[0]
    tm = PROJ_TM
    tn = PROJ_CHUNKS * COL_BLK
    d2, d3 = DIL_PATTERNS[1][1], DIL_PATTERNS[2][1]
    return pl.pallas_call(
        _inproj_kernel,
        grid=(S // tm, IN_WIDTH // tn),
        in_specs=[
            pl.BlockSpec((tm, D_MODEL), lambda i, j: (i, 0)),
            pl.BlockSpec((D_MODEL, tn), lambda i, j: (0, j)),
            pl.BlockSpec((tm, LANES), lambda i, j: (i, 0)),
            pl.BlockSpec((tm, LANES), lambda i, j: (i, 0)),
            pl.BlockSpec((tm, LANES), lambda i, j: (i, 0)),
        ],
        out_specs=[pl.BlockSpec((tm, tn), lambda i, j: (i, jnp.minimum(j, MAIN_TILES - 1))),
                   pl.BlockSpec((d2, tm // d2, tn), lambda i, j: (0, i, 0)),
                   pl.BlockSpec((d3, tm // d3, tn), lambda i, j: (0, i, 0))],
        out_shape=[jax.ShapeDtypeStruct((S, MAIN_TILES * tn), BF16),
                   jax.ShapeDtypeStruct((d2, S // d2, tn), BF16),
                   jax.ShapeDtypeStruct((d3, S // d3, tn), BF16)],
        scratch_shapes=[pltpu.VMEM((tm, D_MODEL), BF16),
                        pltpu.VMEM((COL_BLK // LANES, tm, LANES), F32)],
        compiler_params=_cparams(("arbitrary", "arbitrary")),
        name="inproj",
    )(x2d, w_cat, rc, ra, rb)


def _attn_kernel(q_ref, kp_ref, kc_ref, kn_ref, vp_ref, vc_ref, vn_ref, band_ref, o_ref, l_ref):
    i = pl.program_id(1)
    n = pl.num_programs(1)
    k = jnp.concatenate([kp_ref[0], kc_ref[0], kn_ref[0]], axis=0)
    v = jnp.concatenate([vp_ref[0], vc_ref[0], vn_ref[0]], axis=0)
    hq = ATTN_BLK // 2
    col = lax.broadcasted_iota(jnp.int32, (hq, 2 * ATTN_BLK), 1)
    lane = lax.broadcasted_iota(jnp.int32, (hq, LANES), 1)
    scale = HEAD_DIM ** -0.5
    first_bias = jnp.where(col >= jnp.where(i > 0, 0, ATTN_BLK), band_ref[0], NEG_INF)
    last_bias = jnp.where(col < jnp.where(i < n - 1, 2 * ATTN_BLK, ATTN_BLK), band_ref[1], NEG_INF)

    units = [(s, hf, h) for s in range(ATTN_QB) for hf in range(2) for h in range(HEADS)]

    def scores(s, hf, h):
        qs = slice(s * ATTN_BLK + hf * hq, s * ATTN_BLK + (hf + 1) * hq)
        ks = slice((s + hf) * ATTN_BLK, (s + hf + 2) * ATTN_BLK)
        hs = slice(h * HEAD_DIM, (h + 1) * HEAD_DIM)
        return lax.dot_general(q_ref[0, qs, hs], k[ks, hs], (((1,), (1,)), ((), ())),
                               preferred_element_type=F32)

    pending = [scores(*u) for u in units[:ATTN_AHEAD]]
    lse_tile = None
    for j, (s, hf, h) in enumerate(units):
        raw = pending.pop(0)
        if j + ATTN_AHEAD < len(units):
            pending.append(scores(*units[j + ATTN_AHEAD]))
        if s == 0 and hf == 0:
            bias = first_bias
        elif s == ATTN_QB - 1 and hf == 1:
            bias = last_bias
        else:
            bias = band_ref[hf]
        qs = slice(s * ATTN_BLK + hf * hq, s * ATTN_BLK + (hf + 1) * hq)
        ks = slice((s + hf) * ATTN_BLK, (s + hf + 2) * ATTN_BLK)
        hs = slice(h * HEAD_DIM, (h + 1) * HEAD_DIM)
        sc = raw * scale + bias
        m = jnp.max(sc, axis=-1, keepdims=True)
        p = jnp.exp(sc - m)
        den = jnp.sum(p, axis=-1, keepdims=True)
        o = jnp.dot(p.astype(BF16), v[ks, hs], preferred_element_type=F32) / den
        o_ref[0, qs, hs] = o.astype(BF16)
        if h == 0:
            lse_tile = jnp.zeros((hq, LANES), F32)
        lse_tile = jnp.where(lane == h, m + jnp.log(den), lse_tile)
        if h == HEADS - 1:
            l_ref[0, qs, :] = lse_tile


def _band_bias():
    hq = ATTN_BLK // 2
    out = np.empty((2, hq, 2 * ATTN_BLK), np.float32)
    for hf in range(2):
        i = np.arange(hq)[:, None] + hf * hq
        c = np.arange(2 * ATTN_BLK)[None, :] + hf * ATTN_BLK
        rel = c - ATTN_BLK - i
        out[hf] = np.where(np.abs(rel) <= hq, 0.0, NEG_INF)
    return jnp.asarray(out)


def _attention(arr, qb, kb, vb):
    d, L, _ = arr.shape
    assert ATTN_QB >= 2 and L % (ATTN_QB * ATTN_BLK) == 0
    nb = L // ATTN_BLK
    big = (1, ATTN_QB * ATTN_BLK, GROUP_W)
    small = (1, ATTN_BLK, GROUP_W)
    cur = lambda cb: pl.BlockSpec(big, lambda r, i: (r, i, cb))
    prev = lambda cb: pl.BlockSpec(small, lambda r, i: (r, jnp.maximum(ATTN_QB * i - 1, 0), cb))
    nxt = lambda cb: pl.BlockSpec(small, lambda r, i: (r, jnp.minimum(ATTN_QB * (i + 1), nb - 1), cb))
    return pl.pallas_call(
        _attn_kernel,
        grid=(d, nb // ATTN_QB),
        in_specs=[cur(qb), prev(kb), cur(kb), nxt(kb), prev(vb), cur(vb), nxt(vb),
                  pl.BlockSpec((2, ATTN_BLK // 2, 2 * ATTN_BLK), lambda r, i: (0, 0, 0))],
        out_specs=[pl.BlockSpec(big, lambda r, i: (r, i, 0)),
                   pl.BlockSpec((1, ATTN_QB * ATTN_BLK, LANES), lambda r, i: (r, i, 0))],
        out_shape=[jax.ShapeDtypeStruct((d, L, GROUP_W), BF16),
                   jax.ShapeDtypeStruct((d, L, LANES), F32)],
        compiler_params=_cparams(("arbitrary", "arbitrary")),
        name=f"attn_d{d}",
    )(arr, arr, arr, arr, arr, arr, arr, _band_bias())


def _tail_kernel(z_ref, ga_ref, gb_ref, o1_ref, o2_ref, o3_ref, l1_ref, l2_ref, l3_ref, x_ref,
                 sw_ref, sbias_ref, slg_ref, slb_ref, wso_ref, wao_ref, wo_ref,
                 g1_ref, b1_ref, wr_ref, br_ref, ltri_ref,
                 x1_ref, idx_ref, gate_ref, rank_ref, cnt_ref,
                 carry_ref, on2_ref, ln2_ref, on3_ref, ln3_ref):
    tm = z_ref.shape[0]

    @pl.when(pl.program_id(0) == 0)
    def _():
        carry_ref[...] = jnp.zeros(carry_ref.shape, carry_ref.dtype)

    z = z_ref[...]
    u = z[:, :SGU_WIDTH].astype(F32)
    vn = _layer_norm(z[:, SGU_WIDTH:].astype(F32), slg_ref[...], slb_ref[...]).astype(BF16)
    rows = []
    for c in range(tm // CHUNK):
        cols = []
        for g in range(SGU_GROUPS):
            blk = vn[c * CHUNK:(c + 1) * CHUNK, g * CHUNK:(g + 1) * CHUNK]
            cols.append(jnp.dot(sw_ref[g], blk, preferred_element_type=F32))
        rows.append(jnp.concatenate(cols, axis=1) + sbias_ref[...])
    mixed = jnp.concatenate(rows, axis=0)
    y_b = jnp.dot((u * mixed).astype(BF16), wso_ref[...], preferred_element_type=F32)

    def to_position_order(o_ref, l_ref, onat_ref, lnat_ref):
        d = o_ref.shape[0]
        for r in range(d):
            rows = pl.ds(r, tm // d, stride=d)
            for h in range(HEADS):
                onat_ref[h, rows, :] = o_ref[r, :, h * HEAD_DIM:(h + 1) * HEAD_DIM].astype(F32)
            lnat_ref[rows, :] = l_ref[r]

    to_position_order(o2_ref, l2_ref, on2_ref, ln2_ref)
    to_position_order(o3_ref, l3_ref, on3_ref, ln3_ref)

    l1, l2, l3 = l1_ref[...], ln2_ref[...], ln3_ref[...]
    m = jnp.maximum(jnp.maximum(l1, l2), l3)
    e1, e2, e3 = jnp.exp(l1 - m), jnp.exp(l2 - m), jnp.exp(l3 - m)
    inv = 1.0 / (e1 + e2 + e3)
    w1, w2, w3 = e1 * inv, e2 * inv, e3 * inv
    parts = []
    for h in range(HEADS):
        hs = slice(h * HEAD_DIM, (h + 1) * HEAD_DIM)
        parts.append(w1[:, h:h + 1] * o1_ref[:, hs].astype(F32)
                     + w2[:, h:h + 1] * on2_ref[h]
                     + w3[:, h:h + 1] * on3_ref[h])
    o_a = jnp.concatenate(parts, axis=1).astype(BF16)
    y_a = jnp.dot(o_a, wao_ref[...], preferred_element_type=F32)

    merged = ga_ref[...].astype(F32) * y_a + gb_ref[...].astype(F32) * y_b
    hmix = jnp.dot(merged.astype(BF16), wo_ref[...], preferred_element_type=F32)
    x1 = _layer_norm(DEEPNORM_ALPHA * x_ref[...] + hmix, g1_ref[...], b1_ref[...])
    x1_ref[...] = x1

    logits = jnp.dot(x1.astype(BF16), wr_ref[...], preferred_element_type=F32) + br_ref[...]
    lane = lax.broadcasted_iota(jnp.int32, (tm, LANES), 1)
    lane_f = lane.astype(F32)
    vals = jnp.where(lane < N_EXPERTS, logits, -jnp.inf)
    idx_tile = jnp.zeros((tm, LANES), F32)
    val_tile = jnp.zeros((tm, LANES), F32)
    member = jnp.zeros((tm, LANES), F32)
    top0 = None
    picks = []
    for kk in range(TOP_K):
        mk = jnp.max(vals, axis=-1, keepdims=True)
        ik = jnp.min(jnp.where(vals == mk, lane_f, float(LANES)), axis=-1, keepdims=True)
        hit = lane_f == ik
        vals = jnp.where(hit, -jnp.inf, vals)
        member = jnp.where(hit, 1.0, member)
        picks.append(hit)
        if kk == 0:
            top0 = mk
        idx_tile = jnp.where(lane == kk, ik, idx_tile)
        val_tile = jnp.where(lane == kk, jnp.exp(mk - top0), val_tile)
    idx_ref[...] = idx_tile
    gate_ref[...] = val_tile / jnp.sum(val_tile, axis=-1, keepdims=True)

    before = jnp.dot(ltri_ref[...], member.astype(BF16), preferred_element_type=F32) + carry_ref[0:1, :]
    rank_tile = jnp.zeros((tm, LANES), F32)
    for kk in range(TOP_K):
        rk = jnp.sum(jnp.where(picks[kk], before, 0.0), axis=-1, keepdims=True)
        rank_tile = jnp.where(lane == kk, rk, rank_tile)
    rank_ref[...] = rank_tile
    carry_ref[...] = carry_ref[...] + jnp.sum(member, axis=0, keepdims=True)
    cnt_ref[...] = carry_ref[...]


def _tail(proj, o1, o2, o3, l1, l2, l3, x2d, sw, sbias, slg, slb, wso, wao, wo, g1, b1, wr, br):
    S = x2d.shape[0]
    tm = TAIL_TM
    row = lambda w: pl.BlockSpec((tm, w), lambda i: (i, 0))

    def const(shape):
        nd = len(shape)
        return pl.BlockSpec(shape, lambda i: (0,) * nd, pipeline_mode=pl.Buffered(1))

    wide = lambda cb: pl.BlockSpec((tm, D_MODEL), lambda i: (i, cb))
    res = lambda a: pl.BlockSpec((a.shape[0], tm // a.shape[0], a.shape[2]), lambda i: (0, i, 0))
    r_, c_ = np.arange(tm)[:, None], np.arange(tm)[None, :]
    ltri = jnp.asarray(c_ < r_, BF16)
    return pl.pallas_call(
        _tail_kernel,
        grid=(S // tm,),
        in_specs=[wide(0), wide(1), wide(2),
                  row(GROUP_W), res(o2), res(o3), row(LANES), res(l2), res(l3),
                  row(D_MODEL),
                  const(sw.shape), const(sbias.shape), const(slg.shape), const(slb.shape),
                  const(wso.shape), const(wao.shape), const(wo.shape),
                  const(g1.shape), const(b1.shape), const(wr.shape), const(br.shape), const(ltri.shape)],
        out_specs=[row(D_MODEL), row(LANES), row(LANES), row(LANES),
                   pl.BlockSpec((8, LANES), lambda i: (0, 0))],
        out_shape=[jax.ShapeDtypeStruct((S, D_MODEL), F32),
                   jax.ShapeDtypeStruct((S, LANES), F32),
                   jax.ShapeDtypeStruct((S, LANES), F32),
                   jax.ShapeDtypeStruct((S, LANES), F32),
                   jax.ShapeDtypeStruct((8, LANES), F32)],
        scratch_shapes=[pltpu.VMEM((8, LANES), F32),
                        pltpu.VMEM((HEADS, tm, HEAD_DIM), F32), pltpu.VMEM((tm, LANES), F32),
                        pltpu.VMEM((HEADS, tm, HEAD_DIM), F32), pltpu.VMEM((tm, LANES), F32)],
        compiler_params=_cparams(("arbitrary",)),
        name="tail",
    )(proj, proj, proj, o1, o2, o3, l1, l2, l3, x2d, sw, sbias, slg, slb, wso, wao, wo, g1, b1, wr, br, ltri)


def _row_gather(idx_ref, n, src_hbm, dst, sem):
    def body(r, carry):
        t = idx_ref[0, 0, r]
        pltpu.make_async_copy(src_hbm.at[pl.ds(t, 1)], dst.at[pl.ds(r, 1)], sem).start()
        return carry
    lax.fori_loop(0, n, body, 0, unroll=8)


def _row_gather_unrolled(idx_ref, n, src_hbm, dst, sem):
    for r0 in range(0, n, IDX_BATCH):
        ts = [idx_ref[0, 0, r] for r in range(r0, min(r0 + IDX_BATCH, n))]
        for u, t in enumerate(ts):
            pltpu.make_async_copy(src_hbm.at[pl.ds(t, 1)], dst.at[pl.ds(r0 + u, 1)], sem).start()


def _stream_expert_weights(i, t0_ref, t1_ref, te_ref, nt_ref, w_hbm, stg, wsem, wbf):
    rows = stg.shape[1]
    ntask = nt_ref[0]

    def start(k):
        c = k % W_CHUNKS
        src = w_hbm.at[te_ref[k], pl.ds(pl.multiple_of(c * rows, rows), rows), :]
        pltpu.make_async_copy(src, stg.at[k % 2], wsem.at[k % 2]).start(priority=1)

    @pl.when(i == 0)
    def _():
        start(0)
        start(1)

    def body(k, carry):
        slot = k % 2
        pltpu.make_async_copy(w_hbm.at[0, pl.ds(0, rows), :], stg.at[slot], wsem.at[slot]).wait()
        par = (k // W_CHUNKS) % 2
        c = k % W_CHUNKS
        wbf[par, pl.ds(pl.multiple_of(c * rows, rows), rows), :] = stg[slot].astype(BF16)

        @pl.when(k + 2 < ntask)
        def _():
            start(k + 2)
        return carry

    lax.fori_loop(t0_ref[i], t1_ref[i], body, 0)


def _moe_up_kernel(be_ref, nu_ref, par_ref, t0_ref, t1_ref, te_ref, nt_ref,
                   tok_ref, tok1_ref, tok2_ref, x_hbm, w_hbm, b_ref, h_ref, xbuf, xsem, wbf, stg, wsem):
    i = pl.program_id(0)
    nused = nu_ref[0]
    slot = i % X_SLOTS

    @pl.when(i == 0)
    def _():
        _row_gather(tok_ref, MOE_TM, x_hbm, xbuf.at[0], xsem.at[0])

    @pl.when((i == 0) & (nused > 1))
    def _():
        _row_gather(tok1_ref, MOE_TM, x_hbm, xbuf.at[1], xsem.at[1])

    _stream_expert_weights(i, t0_ref, t1_ref, te_ref, nt_ref, w_hbm, stg, wsem, wbf)

    def compute(gather_ahead):
        if gather_ahead:
            ahead = (i + 2) % X_SLOTS
            _row_gather_unrolled(tok2_ref, MOE_TM, x_hbm, xbuf.at[ahead], xsem.at[ahead])
        pltpu.make_async_copy(x_hbm.at[pl.ds(0, MOE_TM)], xbuf.at[slot], xsem.at[slot]).wait()
        xb = xbuf[slot].astype(BF16)
        w = wbf.at[par_ref[i]]
        for c in range(D_EXPERT // COL_BLK):
            cs = slice(c * COL_BLK, (c + 1) * COL_BLK)
            us = slice(D_EXPERT + c * COL_BLK, D_EXPERT + (c + 1) * COL_BLK)
            g = jnp.dot(xb, w[:, cs], preferred_element_type=F32) + b_ref[0, :, cs]
            up = jnp.dot(xb, w[:, us], preferred_element_type=F32) + b_ref[0, :, us]
            g = jnp.minimum(g, SWIGLU_LIMIT)
            up = jnp.clip(up, -SWIGLU_LIMIT, SWIGLU_LIMIT)
            act = (up + 1.0) * (g * (1.0 / (1.0 + jnp.exp(-SWIGLU_ALPHA * g))))
            h_ref[:, cs] = act.astype(BF16)

    @pl.when(i + 2 < nused)
    def _():
        compute(True)

    @pl.when((i + 2 >= nused) & (i < nused))
    def _():
        compute(False)

    @pl.when(i >= nused)
    def _():
        h_ref[...] = jnp.zeros(h_ref.shape, h_ref.dtype)


def _moe_up(sched, row_tok3, x1, wgu, bgu):
    nblk = sched[0].shape[0]
    rows = D_MODEL // W_CHUNKS
    smem_blk = lambda off: pl.BlockSpec(
        (1, 1, MOE_TM), lambda i, *_: (jnp.minimum(i + off, nblk - 1), 0, 0), memory_space=pltpu.SMEM)
    grid_spec = pltpu.PrefetchScalarGridSpec(
        num_scalar_prefetch=len(sched),
        grid=(nblk,),
        in_specs=[smem_blk(0), smem_blk(1), smem_blk(2),
                  pl.BlockSpec(memory_space=pl.ANY),
                  pl.BlockSpec(memory_space=pl.ANY),
                  pl.BlockSpec((1, 1, 2 * D_EXPERT), lambda i, be, *_: (be[i], 0, 0))],
        out_specs=pl.BlockSpec((MOE_TM, D_EXPERT), lambda i, *_: (i, 0)),
        scratch_shapes=[pltpu.VMEM((X_SLOTS, MOE_TM, D_MODEL), F32), pltpu.SemaphoreType.DMA((X_SLOTS,)),
                        pltpu.VMEM((2, D_MODEL, 2 * D_EXPERT), BF16),
                        pltpu.VMEM((2, rows, 2 * D_EXPERT), F32), pltpu.SemaphoreType.DMA((2,))],
    )
    return pl.pallas_call(
        _moe_up_kernel,
        grid_spec=grid_spec,
        out_shape=jax.ShapeDtypeStruct((nblk * MOE_TM, D_EXPERT), BF16),
        compiler_params=_cparams(("arbitrary",)),
        name="moe_up",
    )(*sched, row_tok3, row_tok3, row_tok3, x1, wgu, bgu)


def _moe_down_kernel(be_ref, nu_ref, par_ref, t0_ref, t1_ref, te_ref, nt_ref,
                     h_ref, w_hbm, b_ref, y_ref, wbf, stg, wsem):
    i = pl.program_id(0)
    _stream_expert_weights(i, t0_ref, t1_ref, te_ref, nt_ref, w_hbm, stg, wsem, wbf)

    @pl.when(i < nu_ref[0])
    def _():
        y_ref[...] = jnp.dot(h_ref[...], wbf[par_ref[i]], preferred_element_type=F32) + b_ref[0]

    @pl.when(i >= nu_ref[0])
    def _():
        y_ref[...] = jnp.zeros(y_ref.shape, y_ref.dtype)


def _moe_down(sched, h, wd, bd):
    nblk = sched[0].shape[0]
    rows = D_EXPERT // W_CHUNKS
    grid_spec = pltpu.PrefetchScalarGridSpec(
        num_scalar_prefetch=len(sched),
        grid=(nblk,),
        in_specs=[pl.BlockSpec((MOE_TM, D_EXPERT), lambda i, be, nu, *_: (jnp.minimum(i, nu[0] - 1), 0)),
                  pl.BlockSpec(memory_space=pl.ANY),
                  pl.BlockSpec((1, 1, D_MODEL), lambda i, be, *_: (be[i], 0, 0))],
        out_specs=pl.BlockSpec((MOE_TM, D_MODEL), lambda i, *_: (i, 0)),
        scratch_shapes=[pltpu.VMEM((2, D_EXPERT, D_MODEL), BF16),
                        pltpu.VMEM((2, rows, D_MODEL), F32), pltpu.SemaphoreType.DMA((2,))],
    )
    return pl.pallas_call(
        _moe_down_kernel,
        grid_spec=grid_spec,
        out_shape=jax.ShapeDtypeStruct((nblk * MOE_TM, D_MODEL), F32),
        compiler_params=_cparams(("arbitrary",)),
        name="moe_down",
    )(*sched, h, wd, bd)


def _combine_kernel(pos_ref, posn_ref, gate_ref, x1_ref, g2_ref, b2_ref, y_hbm, o_ref, ybuf, sem):
    i = pl.program_id(0)
    n = pl.num_programs(0)
    slot = i % 2
    nrow = TOP_K * COMB_TM

    @pl.when(i == 0)
    def _():
        _row_gather(pos_ref, nrow, y_hbm, ybuf.at[0], sem.at[0])

    @pl.when(i + 1 < n)
    def _():
        _row_gather_unrolled(posn_ref, nrow, y_hbm, ybuf.at[1 - slot], sem.at[1 - slot])

    pltpu.make_async_copy(y_hbm.at[pl.ds(0, nrow)], ybuf.at[slot], sem.at[slot]).wait()
    gate = gate_ref[...]
    y = jnp.zeros((COMB_TM, D_MODEL), F32)
    for kk in range(TOP_K):
        y = y + gate[:, kk:kk + 1] * ybuf[slot, kk * COMB_TM:(kk + 1) * COMB_TM, :]
    o_ref[...] = _layer_norm(DEEPNORM_ALPHA * x1_ref[...] + y, g2_ref[...], b2_ref[...])


def _combine(pos3, gate, x1, g2, b2, yb):
    S = x1.shape[0]
    nt = S // COMB_TM
    nrow = TOP_K * COMB_TM
    smem_blk = lambda off: pl.BlockSpec(
        (1, 1, nrow), lambda i: (jnp.minimum(i + off, nt - 1), 0, 0), memory_space=pltpu.SMEM)
    return pl.pallas_call(
        _combine_kernel,
        grid=(nt,),
        in_specs=[smem_blk(0), smem_blk(1),
                  pl.BlockSpec((COMB_TM, LANES), lambda i: (i, 0)),
                  pl.BlockSpec((COMB_TM, D_MODEL), lambda i: (i, 0)),
                  pl.BlockSpec((1, D_MODEL), lambda i: (0, 0)),
                  pl.BlockSpec((1, D_MODEL), lambda i: (0, 0)),
                  pl.BlockSpec(memory_space=pl.ANY)],
        out_specs=pl.BlockSpec((COMB_TM, D_MODEL), lambda i: (i, 0)),
        out_shape=jax.ShapeDtypeStruct((S, D_MODEL), F32),
        scratch_shapes=[pltpu.VMEM((2, nrow, D_MODEL), F32), pltpu.SemaphoreType.DMA((2,))],
        compiler_params=_cparams(("arbitrary",)),
        name="combine",
    )(pos3, pos3, gate, x1, g2, b2, yb)


def _rope_tables(pos):
    half = ROT_DIM // 2
    inv_freq = ROPE_THETA ** (-jnp.arange(0, ROT_DIM, 2, dtype=F32) / ROT_DIM)
    ang = pos.astype(F32)[:, None] * inv_freq
    cos, sin = jnp.cos(ang), jnp.sin(ang)
    S = pos.shape[0]
    one = jnp.ones((S, HEAD_DIM - ROT_DIM), F32)
    zero = jnp.zeros((S, HEAD_DIM - ROT_DIM), F32)
    z16 = jnp.zeros((S, half), F32)
    rc = jnp.concatenate([cos, cos, one], axis=1)
    ra = jnp.concatenate([-sin, z16, zero], axis=1)
    rb = jnp.concatenate([z16, sin, zero], axis=1)
    return rc, ra, rb


def _invert_kernel(dest_ref, pad_lo_ref, pad_hi_ref, rt_ref):
    i = pl.program_id(0)

    @pl.when(i == 0)
    def _():
        for e in range(pad_lo_ref.shape[0]):
            def fill(p, carry):
                rt_ref[p] = 0
                return carry
            lax.fori_loop(pad_lo_ref[e], pad_hi_ref[e], fill, 0)

    base = i * INV_BLK

    def body(jb, carry):
        j0 = jb * IDX_BATCH
        ds = [dest_ref[0, 0, j0 + u] for u in range(IDX_BATCH)]
        for u, d in enumerate(ds):
            rt_ref[d] = lax.shift_right_logical(base + j0 + u, TOP_K.bit_length() - 1)
        return carry
    lax.fori_loop(0, INV_BLK // IDX_BATCH, body, 0)


def _invert_rows(dest, pad_lo, pad_hi, nrows):
    n = dest.size
    return pl.pallas_call(
        _invert_kernel,
        grid=(n // INV_BLK,),
        in_specs=[pl.BlockSpec((1, 1, INV_BLK), lambda i: (i, 0, 0), memory_space=pltpu.SMEM),
                  pl.BlockSpec(memory_space=pltpu.SMEM),
                  pl.BlockSpec(memory_space=pltpu.SMEM)],
        out_specs=pl.BlockSpec(memory_space=pltpu.SMEM),
        out_shape=jax.ShapeDtypeStruct((nrows,), jnp.int32),
        compiler_params=_cparams(("arbitrary",)),
        name="invert_rows",
    )(dest.reshape(n // INV_BLK, 1, INV_BLK), pad_lo, pad_hi)


def _routing(idx, rank, counts, S):
    n = S * TOP_K
    nblk = n // MOE_TM + N_EXPERTS
    onehot = (idx[:, :, None] == jnp.arange(N_EXPERTS, dtype=jnp.int32)[None, None, :])
    padded = (counts + MOE_TM - 1) // MOE_TM * MOE_TM
    pends = jnp.cumsum(padded)
    pstarts = pends - padded
    dest = jnp.sum(jnp.where(onehot, pstarts[None, None, :], 0), axis=-1) + rank
    nrows = nblk * MOE_TM
    pad_lo = jnp.concatenate([pstarts + counts, pends[-1:]]).astype(jnp.int32)
    pad_hi = jnp.concatenate([pends, jnp.full((1,), nrows, pends.dtype)]).astype(jnp.int32)
    row_tok = _invert_rows(dest.astype(jnp.int32), pad_lo, pad_hi, nrows)
    starts = jnp.arange(nblk, dtype=jnp.int32) * MOE_TM
    block_e = jnp.minimum(jnp.sum(pends[None, :] <= starts[:, None], axis=1, dtype=jnp.int32), N_EXPERTS - 1)
    nused = (pends[-1:] // MOE_TM).astype(jnp.int32)

    e_ids = jnp.arange(N_EXPERTS, dtype=jnp.int32)
    nb_e = (padded // MOE_TM).astype(jnp.int32)
    bstart_e = (pstarts // MOE_TM).astype(jnp.int32)
    nonempty = nb_e > 0
    run_of_e = jnp.cumsum(nonempty.astype(jnp.int32)) - 1
    nruns = jnp.sum(nonempty.astype(jnp.int32))
    run_e = jnp.sum(jnp.where(nonempty[None, :] & (run_of_e[None, :] == e_ids[:, None]), e_ids[None, :], 0), axis=1)
    task_e = jnp.repeat(run_e, W_CHUNKS).astype(jnp.int32)
    ntask = (W_CHUNKS * nruns).reshape(1).astype(jnp.int32)
    blk = jnp.arange(nblk, dtype=jnp.int32)
    used = blk < nused[0]
    of_block = block_e[:, None] == e_ids[None, :]
    pick = lambda v: jnp.sum(jnp.where(of_block, v[None, :], 0), axis=1)
    r_i = pick(run_of_e)
    j_i = blk - pick(bstart_e)
    n_i = jnp.maximum(pick(nb_e), 1)
    steps = jnp.arange(1, W_CHUNKS + 1, dtype=jnp.int32)[None, :]
    share = lambda j: jnp.sum(steps * n_i[:, None] <= W_CHUNKS * j[:, None], axis=1, dtype=jnp.int32)
    t0 = W_CHUNKS * (r_i + 1) + share(j_i)
    t1 = W_CHUNKS * (r_i + 1) + share(j_i + 1)
    t0 = jnp.where(blk == 0, 0, jnp.where(used, jnp.minimum(t0, ntask[0]), ntask[0])).astype(jnp.int32)
    t1 = jnp.where(used, jnp.minimum(t1, ntask[0]), ntask[0]).astype(jnp.int32)
    par = (r_i % 2).astype(jnp.int32)
    sched = (block_e, nused, par, t0, t1, task_e, ntask)
    return dest.astype(jnp.int32), row_tok, sched


def kernel(x, positions, w_in, w_attn_out, sgu_ln_g, sgu_ln_b, sgu_w, sgu_b, w_sgu_out, w_out,
           ln1_g, ln1_b, w_router, b_router, w_gate_up, b_gate_up, w_down, b_down, ln2_g, ln2_b):
    B, S, D = x.shape
    assert B == 1 and D == D_MODEL and w_in.shape[0] == DEPTH
    assert S % PROJ_TM == 0 and S % (DIL_PATTERNS[-1][1] * ATTN_QB * ATTN_BLK) == 0
    x2d = x.reshape(S, D)

    a0, a1, a2 = ATTN_WIDTH, 2 * ATTN_WIDTH, 3 * ATTN_WIDTH
    z1 = a2 + 2 * SGU_WIDTH
    wi = w_in[0]
    qkv_cols = [wi[:, p * ATTN_WIDTH + gi * GROUP_W:p * ATTN_WIDTH + (gi + 1) * GROUP_W]
                for gi in range(N_GROUPS) for p in range(3)]
    w_cat = jnp.concatenate([wi[:, a2:z1], wi[:, z1:z1 + D], wi[:, z1 + D:]] + qkv_cols, axis=1).astype(BF16)
    rc, ra, rb = _rope_tables(positions[0])

    main, qkv2, qkv3 = _inproj(x2d, w_cat, rc, ra, rb)
    for window, d in DIL_PATTERNS:
        assert window // (2 * d) == ATTN_BLK // 2
    o1, l1 = _attention(main.reshape(1, S, main.shape[1]), QKV_BLK0, QKV_BLK0 + 1, QKV_BLK0 + 2)
    o2, l2 = _attention(qkv2, 0, 1, 2)
    o3, l3 = _attention(qkv3, 0, 1, 2)

    sbias = jnp.repeat(sgu_b[0].T, CHUNK, axis=1)
    wr = jnp.pad(w_router[0], ((0, 0), (0, LANES - N_EXPERTS))).astype(BF16)
    br = jnp.pad(b_router[0], (0, LANES - N_EXPERTS)).reshape(1, LANES)
    x1, idx_f, gate, rank_f, cnt_f = _tail(
        main, o1.reshape(S, GROUP_W), o2, o3, l1.reshape(S, LANES), l2, l3, x2d,
        sgu_w[0].astype(BF16), sbias, sgu_ln_g[0].reshape(1, -1), sgu_ln_b[0].reshape(1, -1),
        w_sgu_out[0].astype(BF16), w_attn_out[0].astype(BF16), w_out[0].astype(BF16),
        ln1_g[0].reshape(1, -1), ln1_b[0].reshape(1, -1), wr, br)

    idx = idx_f[:, :TOP_K].astype(jnp.int32)
    rank = rank_f[:, :TOP_K].astype(jnp.int32)
    counts = cnt_f[0, :N_EXPERTS].astype(jnp.int32)
    dest, row_tok, sched = _routing(idx, rank, counts, S)
    nblk = sched[0].shape[0]

    h = _moe_up(sched, row_tok.reshape(nblk, 1, MOE_TM), x1, w_gate_up[0], b_gate_up[0].reshape(N_EXPERTS, 1, -1))
    yb = _moe_down(sched, h, w_down[0], b_down[0].reshape(N_EXPERTS, 1, -1))

    nt = S // COMB_TM
    pos3 = dest.reshape(nt, COMB_TM, TOP_K).transpose(0, 2, 1).reshape(nt, 1, TOP_K * COMB_TM)
    out = _combine(pos3, gate, x1, ln2_g[0].reshape(1, -1), ln2_b[0].reshape(1, -1), yb)
    return out.reshape(B, S, D)
```

```python
import numpy as np
import jax
import jax.numpy as jnp
from jax import lax
from jax.experimental import pallas as pl
from jax.experimental.pallas import tpu as pltpu

F32 = jnp.float32
BF16 = jnp.bfloat16

D_MODEL = 2048
HEAD_DIM = 128
ROT_DIM = HEAD_DIM // 4
ROPE_THETA = 500000.0
DIL_PATTERNS = ((128, 1), (512, 4), (2048, 16))
N_GROUPS = len(DIL_PATTERNS)
HEADS = 4
GROUP_W = HEADS * HEAD_DIM
ATTN_WIDTH = N_GROUPS * GROUP_W
CHUNK = 128
SGU_GROUPS = 8
SGU_WIDTH = SGU_GROUPS * CHUNK
IN_WIDTH = 3 * ATTN_WIDTH + 2 * SGU_WIDTH + 2 * D_MODEL
N_EXPERTS = 32
TOP_K = 4
D_EXPERT = D_MODEL
SWIGLU_LIMIT = 7.0
SWIGLU_ALPHA = 1.702
LN_EPS = 1e-5
DEPTH = 1
DEEPNORM_ALPHA = (2 * DEPTH) ** 0.25
NEG_INF = -1e30

LANES = 128
V7X_VMEM_BYTES = 64 * 1024 * 1024
VMEM_LIMIT = V7X_VMEM_BYTES - 6 * 1024 * 1024

COL_BLK = 512
PROJ_TM = 1024
PROJ_CHUNKS = 3
ATTN_BLK = 128
ATTN_QB = 8
ATTN_AHEAD = 10
TAIL_TM = 256
MOE_TM = 256
W_CHUNKS = 8
X_SLOTS = 3
COMB_TM = 256
INV_BLK = 4096
IDX_BATCH = 16

Z_BLK0, GA_BLK0, GB_BLK0, QKV_BLK0 = 0, 4, 8, 12
N_PROJ_BLKS = IN_WIDTH // COL_BLK
MAIN_TILES = 5
assert N_PROJ_BLKS == 21 and N_PROJ_BLKS == (MAIN_TILES + 2) * PROJ_CHUNKS and QKV_BLK0 == (MAIN_TILES - 1) * PROJ_CHUNKS


def _cparams(sem):
    return pltpu.CompilerParams(dimension_semantics=sem, vmem_limit_bytes=VMEM_LIMIT)


def _layer_norm(v, g, b):
    mu = jnp.mean(v, axis=-1, keepdims=True)
    c = v - mu
    var = jnp.mean(c * c, axis=-1, keepdims=True)
    return c * lax.rsqrt(var + LN_EPS) * g + b


def _epilogue_kind(blk):
    if blk < GA_BLK0:
        return "gelu"
    if blk < QKV_BLK0:
        return "sigmoid"
    return "plain" if (blk - QKV_BLK0) % 3 == 2 else "rope"


def _inproj_kernel(x_ref, w_ref, rc_ref, ra_ref, rb_ref, o_ref, g2_ref, g3_ref, xb_ref, racc_ref):
    j = pl.program_id(1)
    tm = x_ref.shape[0]

    def store(c, val, residue_major):
        cs = slice(c * COL_BLK, (c + 1) * COL_BLK)
        if residue_major is None:
            o_ref[:, cs] = val.astype(BF16)
            return
        ref, d = residue_major
        for k in range(COL_BLK // LANES):
            racc_ref[k] = val[:, k * LANES:(k + 1) * LANES]
        for r in range(d):
            parts = [racc_ref[k, pl.ds(r, tm // d, stride=d), :] for k in range(COL_BLK // LANES)]
            ref[r, :, cs] = jnp.concatenate(parts, axis=1).astype(BF16)

    @pl.when(j == 0)
    def _():
        xb_ref[...] = x_ref[...].astype(BF16)

    def epilogue(acc, kind):
        if kind == "gelu":
            return 0.5 * acc * (1.0 + lax.erf(acc * (2.0 ** -0.5)))
        if kind == "sigmoid":
            return 1.0 / (1.0 + jnp.exp(-acc))
        if kind == "rope":
            c = jnp.concatenate([rc_ref[...]] * HEADS, axis=1)
            a = jnp.concatenate([ra_ref[...]] * HEADS, axis=1)
            b = jnp.concatenate([rb_ref[...]] * HEADS, axis=1)
            half = ROT_DIM // 2
            up = pltpu.roll(acc, COL_BLK - half, axis=1)
            dn = pltpu.roll(acc, half, axis=1)
            return acc * c + up * a + dn * b
        return acc

    dests = {MAIN_TILES: (g2_ref, DIL_PATTERNS[1][1]), MAIN_TILES + 1: (g3_ref, DIL_PATTERNS[2][1])}
    patterns = {}
    for t in range(N_PROJ_BLKS // PROJ_CHUNKS):
        kinds = tuple(_epilogue_kind(t * PROJ_CHUNKS + c) for c in range(PROJ_CHUNKS))
        patterns.setdefault((kinds, t if t in dests else None), []).append(t)
    for (kinds, special), tiles in patterns.items():
        cond = j == tiles[0]
        for t in tiles[1:]:
            cond = cond | (j == t)

        @pl.when(cond)
        def _(kinds=kinds, special=special):
            for c, kind in enumerate(kinds):
                cs = slice(c * COL_BLK, (c + 1) * COL_BLK)
                acc = jnp.dot(xb_ref[...], w_ref[:, cs], preferred_element_type=F32)
                store(c, epilogue(acc, kind), dests.get(special))


def _inproj(x2d, w_cat, rc, ra, rb):
    S = x2d.shape[0]
    tm = PROJ_TM
    tn = PROJ_CHUNKS * COL_BLK
    d2, d3 = DIL_PATTERNS[1][1], DIL_PATTERNS[2][1]
    return pl.pallas_call(
        _inproj_kernel,
        grid=(S // tm, IN_WIDTH // tn),
        in_specs=[
            pl.BlockSpec((tm, D_MODEL), lambda i, j: (i, 0)),
            pl.BlockSpec((D_MODEL, tn), lambda i, j: (0, j)),
            pl.BlockSpec((tm, LANES), lambda i, j: (i, 0)),
            pl.BlockSpec((tm, LANES), lambda i, j: (i, 0)),
            pl.BlockSpec((tm, LANES), lambda i, j: (i, 0)),
        ],
        out_specs=[pl.BlockSpec((tm, tn), lambda i, j: (i, jnp.minimum(j, MAIN_TILES - 1))),
                   pl.BlockSpec((d2, tm // d2, tn), lambda i, j: (0, i, 0)),
                   pl.BlockSpec((d3, tm // d3, tn), lambda i, j: (0, i, 0))],
        out_shape=[jax.ShapeDtypeStruct((S, MAIN_TILES * tn), BF16),
                   jax.ShapeDtypeStruct((d2, S // d2, tn), BF16),
                   jax.ShapeDtypeStruct((d3, S // d3, tn), BF16)],
        scratch_shapes=[pltpu.VMEM((tm, D_MODEL), BF16),
                        pltpu.VMEM((COL_BLK // LANES, tm, LANES), F32)],
        compiler_params=_cparams(("arbitrary", "arbitrary")),
        name="inproj",
    )(x2d, w_cat, rc, ra, rb)


def _attn_kernel(q_ref, kp_ref, kc_ref, kn_ref, vp_ref, vc_ref, vn_ref, band_ref, o_ref, l_ref):
    i = pl.program_id(1)
    n = pl.num_programs(1)
    k = jnp.concatenate([kp_ref[0], kc_ref[0], kn_ref[0]], axis=0)
    v = jnp.concatenate([vp_ref[0], vc_ref[0], vn_ref[0]], axis=0)
    hq = ATTN_BLK // 2
    col = lax.broadcasted_iota(jnp.int32, (hq, 2 * ATTN_BLK), 1)
    lane = lax.broadcasted_iota(jnp.int32, (hq, LANES), 1)
    scale = HEAD_DIM ** -0.5
    first_bias = jnp.where(col >= jnp.where(i > 0, 0, ATTN_BLK), band_ref[0], NEG_INF)
    last_bias = jnp.where(col < jnp.where(i < n - 1, 2 * ATTN_BLK, ATTN_BLK), band_ref[1], NEG_INF)

    units = [(s, hf, h) for s in range(ATTN_QB) for hf in range(2) for h in range(HEADS)]

    def scores(s, hf, h):
        qs = slice(s * ATTN_BLK + hf * hq, s * ATTN_BLK + (hf + 1) * hq)
        ks = slice((s + hf) * ATTN_BLK, (s + hf + 2) * ATTN_BLK)
        hs = slice(h * HEAD_DIM, (h + 1) * HEAD_DIM)
        return lax.dot_general(q_ref[0, qs, hs], k[ks, hs], (((1,), (1,)), ((), ())),
                               preferred_element_type=F32)

    pending = [scores(*u) for u in units[:ATTN_AHEAD]]
    lse_tile = None
    for j, (s, hf, h) in enumerate(units):
        raw = pending.pop(0)
        if j + ATTN_AHEAD < len(units):
            pending.append(scores(*units[j + ATTN_AHEAD]))
        if s == 0 and hf == 0:
            bias = first_bias
        elif s == ATTN_QB - 1 and hf == 1:
            bias = last_bias
        else:
            bias = band_ref[hf]
        qs = slice(s * ATTN_BLK + hf * hq, s * ATTN_BLK + (hf + 1) * hq)
        ks = slice((s + hf) * ATTN_BLK, (s + hf + 2) * ATTN_BLK)
        hs = slice(h * HEAD_DIM, (h + 1) * HEAD_DIM)
        sc = raw * scale + bias
        m = jnp.max(sc, axis=-1, keepdims=True)
        p = jnp.exp(sc - m)
        den = jnp.sum(p, axis=-1, keepdims=True)
        o = jnp.dot(p.astype(BF16), v[ks, hs], preferred_element_type=F32) / den
        o_ref[0, qs, hs] = o.astype(BF16)
        if h == 0:
            lse_tile = jnp.zeros((hq, LANES), F32)
        lse_tile = jnp.where(lane == h, m + jnp.log(den), lse_tile)
        if h == HEADS - 1:
            l_ref[0, qs, :] = lse_tile


def _band_bias():
    hq = ATTN_BLK // 2
    out = np.empty((2, hq, 2 * ATTN_BLK), np.float32)
    for hf in range(2):
        i = np.arange(hq)[:, None] + hf * hq
        c = np.arange(2 * ATTN_BLK)[None, :] + hf * ATTN_BLK
        rel = c - ATTN_BLK - i
        out[hf] = np.where(np.abs(rel) <= hq, 0.0, NEG_INF)
    return jnp.asarray(out)


def _attention(arr, qb, kb, vb):
    d, L, _ = arr.shape
    assert ATTN_QB >= 2 and L % (ATTN_QB * ATTN_BLK) == 0
    nb = L // ATTN_BLK
    big = (1, ATTN_QB * ATTN_BLK, GROUP_W)
    small = (1, ATTN_BLK, GROUP_W)
    cur = lambda cb: pl.BlockSpec(big, lambda r, i: (r, i, cb))
    prev = lambda cb: pl.BlockSpec(small, lambda r, i: (r, jnp.maximum(ATTN_QB * i - 1, 0), cb))
    nxt = lambda cb: pl.BlockSpec(small, lambda r, i: (r, jnp.minimum(ATTN_QB * (i + 1), nb - 1), cb))
    return pl.pallas_call(
        _attn_kernel,
        grid=(d, nb // ATTN_QB),
        in_specs=[cur(qb), prev(kb), cur(kb), nxt(kb), prev(vb), cur(vb), nxt(vb),
                  pl.BlockSpec((2, ATTN_BLK // 2, 2 * ATTN_BLK), lambda r, i: (0, 0, 0))],
        out_specs=[pl.BlockSpec(big, lambda r, i: (r, i, 0)),
                   pl.BlockSpec((1, ATTN_QB * ATTN_BLK, LANES), lambda r, i: (r, i, 0))],
        out_shape=[jax.ShapeDtypeStruct((d, L, GROUP_W), BF16),
                   jax.ShapeDtypeStruct((d, L, LANES), F32)],
        compiler_params=_cparams(("arbitrary", "arbitrary")),
        name=f"attn_d{d}",
    )(arr, arr, arr, arr, arr, arr, arr, _band_bias())


def _tail_kernel(z_ref, ga_ref, gb_ref, o1_ref, o2_ref, o3_ref, l1_ref, l2_ref, l3_ref, x_ref,
                 sw_ref, sbias_ref, slg_ref, slb_ref, wso_ref, wao_ref, wo_ref,
                 g1_ref, b1_ref, wr_ref, br_ref, ltri_ref,
                 x1_ref, idx_ref, gate_ref, rank_ref, cnt_ref,
                 carry_ref, on2_ref, ln2_ref, on3_ref, ln3_ref):
    tm = z_ref.shape[0]

    @pl.when(pl.program_id(0) == 0)
    def _():
        carry_ref[...] = jnp.zeros(carry_ref.shape, carry_ref.dtype)

    z = z_ref[...]
    u = z[:, :SGU_WIDTH].astype(F32)
    vn = _layer_norm(z[:, SGU_WIDTH:].astype(F32), slg_ref[...], slb_ref[...]).astype(BF16)
    rows = []
    for c in range(tm // CHUNK):
        cols = []
        for g in range(SGU_GROUPS):
            blk = vn[c * CHUNK:(c + 1) * CHUNK, g * CHUNK:(g + 1) * CHUNK]
            cols.append(jnp.dot(sw_ref[g], blk, preferred_element_type=F32))
        rows.append(jnp.concatenate(cols, axis=1) + sbias_ref[...])
    mixed = jnp.concatenate(rows, axis=0)
    y_b = jnp.dot((u * mixed).astype(BF16), wso_ref[...], preferred_element_type=F32)

    def to_position_order(o_ref, l_ref, onat_ref, lnat_ref):
        d = o_ref.shape[0]
        for r in range(d):
            rows = pl.ds(r, tm // d, stride=d)
            for h in range(HEADS):
                onat_ref[h, rows, :] = o_ref[r, :, h * HEAD_DIM:(h + 1) * HEAD_DIM].astype(F32)
            lnat_ref[rows, :] = l_ref[r]

    to_position_order(o2_ref, l2_ref, on2_ref, ln2_ref)
    to_position_order(o3_ref, l3_ref, on3_ref, ln3_ref)

    l1, l2, l3 = l1_ref[...], ln2_ref[...], ln3_ref[...]
    m = jnp.maximum(jnp.maximum(l1, l2), l3)
    e1, e2, e3 = jnp.exp(l1 - m), jnp.exp(l2 - m), jnp.exp(l3 - m)
    inv = 1.0 / (e1 + e2 + e3)
    w1, w2, w3 = e1 * inv, e2 * inv, e3 * inv
    parts = []
    for h in range(HEADS):
        hs = slice(h * HEAD_DIM, (h + 1) * HEAD_DIM)
        parts.append(w1[:, h:h + 1] * o1_ref[:, hs].astype(F32)
                     + w2[:, h:h + 1] * on2_ref[h]
                     + w3[:, h:h + 1] * on3_ref[h])
    o_a = jnp.concatenate(parts, axis=1).astype(BF16)
    y_a = jnp.dot(o_a, wao_ref[...], preferred_element_type=F32)

    merged = ga_ref[...].astype(F32) * y_a + gb_ref[...].astype(F32) * y_b
    hmix = jnp.dot(merged.astype(BF16), wo_ref[...], preferred_element_type=F32)
    x1 = _layer_norm(DEEPNORM_ALPHA * x_ref[...] + hmix, g1_ref[...], b1_ref[...])
    x1_ref[...] = x1

    logits = jnp.dot(x1.astype(BF16), wr_ref[...], preferred_element_type=F32) + br_ref[...]
    lane = lax.broadcasted_iota(jnp.int32, (tm, LANES), 1)
    lane_f = lane.astype(F32)
    vals = jnp.where(lane < N_EXPERTS, logits, -jnp.inf)
    idx_tile = jnp.zeros((tm, LANES), F32)
    val_tile = jnp.zeros((tm, LANES), F32)
    member = jnp.zeros((tm, LANES), F32)
    top0 = None
    picks = []
    for kk in range(TOP_K):
        mk = jnp.max(vals, axis=-1, keepdims=True)
        ik = jnp.min(jnp.where(vals == mk, lane_f, float(LANES)), axis=-1, keepdims=True)
        hit = lane_f == ik
        vals = jnp.where(hit, -jnp.inf, vals)
        member = jnp.where(hit, 1.0, member)
        picks.append(hit)
        if kk == 0:
            top0 = mk
        idx_tile = jnp.where(lane == kk, ik, idx_tile)
        val_tile = jnp.where(lane == kk, jnp.exp(mk - top0), val_tile)
    idx_ref[...] = idx_tile
    gate_ref[...] = val_tile / jnp.sum(val_tile, axis=-1, keepdims=True)

    before = jnp.dot(ltri_ref[...], member.astype(BF16), preferred_element_type=F32) + carry_ref[0:1, :]
    rank_tile = jnp.zeros((tm, LANES), F32)
    for kk in range(TOP_K):
        rk = jnp.sum(jnp.where(picks[kk], before, 0.0), axis=-1, keepdims=True)
        rank_tile = jnp.where(lane == kk, rk, rank_tile)
    rank_ref[...] = rank_tile
    carry_ref[...] = carry_ref[...] + jnp.sum(member, axis=0, keepdims=True)
    cnt_ref[...] = carry_ref[...]


def _tail(proj, o1, o2, o3, l1, l2, l3, x2d, sw, sbias, slg, slb, wso, wao, wo, g1, b1, wr, br):
    S = x2d.shape[0]
    tm = TAIL_TM
    row = lambda w: pl.BlockSpec((tm, w), lambda i: (i, 0))

    def const(shape):
        nd = len(shape)
        return pl.BlockSpec(shape, lambda i: (0,) * nd, pipeline_mode=pl.Buffered(1))

    wide = lambda cb: pl.BlockSpec((tm, D_MODEL), lambda i: (i, cb))
    res = lambda a: pl.BlockSpec((a.shape[0], tm // a.shape[0], a.shape[2]), lambda i: (0, i, 0))
    r_, c_ = np.arange(tm)[:, None], np.arange(tm)[None, :]
    ltri = jnp.asarray(c_ < r_, BF16)
    return pl.pallas_call(
        _tail_kernel,
        grid=(S // tm,),
        in_specs=[wide(0), wide(1), wide(2),
                  row(GROUP_W), res(o2), res(o3), row(LANES), res(l2), res(l3),
                  row(D_MODEL),
                  const(sw.shape), const(sbias.shape), const(slg.shape), const(slb.shape),
                  const(wso.shape), const(wao.shape), const(wo.shape),
                  const(g1.shape), const(b1.shape), const(wr.shape), const(br.shape), const(ltri.shape)],
        out_specs=[row(D_MODEL), row(LANES), row(LANES), row(LANES),
                   pl.BlockSpec((8, LANES), lambda i: (0, 0))],
        out_shape=[jax.ShapeDtypeStruct((S, D_MODEL), F32),
                   jax.ShapeDtypeStruct((S, LANES), F32),
                   jax.ShapeDtypeStruct((S, LANES), F32),
                   jax.ShapeDtypeStruct((S, LANES), F32),
                   jax.ShapeDtypeStruct((8, LANES), F32)],
        scratch_shapes=[pltpu.VMEM((8, LANES), F32),
                        pltpu.VMEM((HEADS, tm, HEAD_DIM), F32), pltpu.VMEM((tm, LANES), F32),
                        pltpu.VMEM((HEADS, tm, HEAD_DIM), F32), pltpu.VMEM((tm, LANES), F32)],
        compiler_params=_cparams(("arbitrary",)),
        name="tail",
    )(proj, proj, proj, o1, o2, o3, l1, l2, l3, x2d, sw, sbias, slg, slb, wso, wao, wo, g1, b1, wr, br, ltri)


def _row_gather(idx_ref, n, src_hbm, dst, sem):
    def body(r, carry):
        t = idx_ref[0, 0, r]
        pltpu.make_async_copy(src_hbm.at[pl.ds(t, 1)], dst.at[pl.ds(r, 1)], sem).start()
        return carry
    lax.fori_loop(0, n, body, 0, unroll=8)


def _row_gather_unrolled(idx_ref, n, src_hbm, dst, sem, both_queues=False):
    for r0 in range(0, n, IDX_BATCH):
        ts = [idx_ref[0, 0, r] for r in range(r0, min(r0 + IDX_BATCH, n))]
        for u, t in enumerate(ts):
            pltpu.make_async_copy(src_hbm.at[pl.ds(t, 1)], dst.at[pl.ds(r0 + u, 1)], sem).start(
                priority=(u % 2) if both_queues else 0)


def _stream_expert_weights(i, t0_ref, t1_ref, te_ref, nt_ref, w_hbm, stg, wsem, wbf):
    rows = stg.shape[1]
    ntask = nt_ref[0]

    def start(k):
        c = k % W_CHUNKS
        src = w_hbm.at[te_ref[k], pl.ds(pl.multiple_of(c * rows, rows), rows), :]
        pltpu.make_async_copy(src, stg.at[k % 2], wsem.at[k % 2]).start(priority=1)

    @pl.when(i == 0)
    def _():
        start(0)
        start(1)

    def body(k, carry):
        slot = k % 2
        pltpu.make_async_copy(w_hbm.at[0, pl.ds(0, rows), :], stg.at[slot], wsem.at[slot]).wait()
        par = (k // W_CHUNKS) % 2
        c = k % W_CHUNKS
        wbf[par, pl.ds(pl.multiple_of(c * rows, rows), rows), :] = stg[slot].astype(BF16)

        @pl.when(k + 2 < ntask)
        def _():
            start(k + 2)
        return carry

    lax.fori_loop(t0_ref[i], t1_ref[i], body, 0)


def _moe_up_kernel(be_ref, nu_ref, par_ref, t0_ref, t1_ref, te_ref, nt_ref,
                   tok_ref, tok1_ref, tok2_ref, x_hbm, w_hbm, b_ref, h_ref, xbuf, xsem, wbf, stg, wsem):
    i = pl.program_id(0)
    nused = nu_ref[0]
    slot = i % X_SLOTS

    @pl.when(i == 0)
    def _():
        _row_gather(tok_ref, MOE_TM, x_hbm, xbuf.at[0], xsem.at[0])

    @pl.when((i == 0) & (nused > 1))
    def _():
        _row_gather(tok1_ref, MOE_TM, x_hbm, xbuf.at[1], xsem.at[1])

    _stream_expert_weights(i, t0_ref, t1_ref, te_ref, nt_ref, w_hbm, stg, wsem, wbf)

    def compute(gather_ahead):
        if gather_ahead:
            ahead = (i + 2) % X_SLOTS
            _row_gather_unrolled(tok2_ref, MOE_TM, x_hbm, xbuf.at[ahead], xsem.at[ahead])
        pltpu.make_async_copy(x_hbm.at[pl.ds(0, MOE_TM)], xbuf.at[slot], xsem.at[slot]).wait()
        xb = xbuf[slot].astype(BF16)
        w = wbf.at[par_ref[i]]
        for c in range(D_EXPERT // COL_BLK):
            cs = slice(c * COL_BLK, (c + 1) * COL_BLK)
            us = slice(D_EXPERT + c * COL_BLK, D_EXPERT + (c + 1) * COL_BLK)
            g = jnp.dot(xb, w[:, cs], preferred_element_type=F32) + b_ref[0, :, cs]
            up = jnp.dot(xb, w[:, us], preferred_element_type=F32) + b_ref[0, :, us]
            g = jnp.minimum(g, SWIGLU_LIMIT)
            up = jnp.clip(up, -SWIGLU_LIMIT, SWIGLU_LIMIT)
            act = (up + 1.0) * (g * (1.0 / (1.0 + jnp.exp(-SWIGLU_ALPHA * g))))
            h_ref[:, cs] = act.astype(BF16)

    @pl.when(i + 2 < nused)
    def _():
        compute(True)

    @pl.when((i + 2 >= nused) & (i < nused))
    def _():
        compute(False)

    @pl.when(i >= nused)
    def _():
        h_ref[...] = jnp.zeros(h_ref.shape, h_ref.dtype)


def _moe_up(sched, row_tok3, x1, wgu, bgu):
    nblk = sched[0].shape[0]
    rows = D_MODEL // W_CHUNKS
    smem_blk = lambda off: pl.BlockSpec(
        (1, 1, MOE_TM), lambda i, *_: (jnp.minimum(i + off, nblk - 1), 0, 0), memory_space=pltpu.SMEM)
    grid_spec = pltpu.PrefetchScalarGridSpec(
        num_scalar_prefetch=len(sched),
        grid=(nblk,),
        in_specs=[smem_blk(0), smem_blk(1), smem_blk(2),
                  pl.BlockSpec(memory_space=pl.ANY),
                  pl.BlockSpec(memory_space=pl.ANY),
                  pl.BlockSpec((1, 1, 2 * D_EXPERT), lambda i, be, *_: (be[i], 0, 0))],
        out_specs=pl.BlockSpec((MOE_TM, D_EXPERT), lambda i, *_: (i, 0)),
        scratch_shapes=[pltpu.VMEM((X_SLOTS, MOE_TM, D_MODEL), F32), pltpu.SemaphoreType.DMA((X_SLOTS,)),
                        pltpu.VMEM((2, D_MODEL, 2 * D_EXPERT), BF16),
                        pltpu.VMEM((2, rows, 2 * D_EXPERT), F32), pltpu.SemaphoreType.DMA((2,))],
    )
    return pl.pallas_call(
        _moe_up_kernel,
        grid_spec=grid_spec,
        out_shape=jax.ShapeDtypeStruct((nblk * MOE_TM, D_EXPERT), BF16),
        compiler_params=_cparams(("arbitrary",)),
        name="moe_up",
    )(*sched, row_tok3, row_tok3, row_tok3, x1, wgu, bgu)


def _moe_down_kernel(be_ref, nu_ref, par_ref, t0_ref, t1_ref, te_ref, nt_ref,
                     h_ref, w_hbm, b_ref, y_ref, wbf, stg, wsem):
    i = pl.program_id(0)
    _stream_expert_weights(i, t0_ref, t1_ref, te_ref, nt_ref, w_hbm, stg, wsem, wbf)

    @pl.when(i < nu_ref[0])
    def _():
        y_ref[...] = jnp.dot(h_ref[...], wbf[par_ref[i]], preferred_element_type=F32) + b_ref[0]

    @pl.when(i >= nu_ref[0])
    def _():
        y_ref[...] = jnp.zeros(y_ref.shape, y_ref.dtype)


def _moe_down(sched, h, wd, bd):
    nblk = sched[0].shape[0]
    rows = D_EXPERT // W_CHUNKS
    grid_spec = pltpu.PrefetchScalarGridSpec(
        num_scalar_prefetch=len(sched),
        grid=(nblk,),
        in_specs=[pl.BlockSpec((MOE_TM, D_EXPERT), lambda i, be, nu, *_: (jnp.minimum(i, nu[0] - 1), 0)),
                  pl.BlockSpec(memory_space=pl.ANY),
                  pl.BlockSpec((1, 1, D_MODEL), lambda i, be, *_: (be[i], 0, 0))],
        out_specs=pl.BlockSpec((MOE_TM, D_MODEL), lambda i, *_: (i, 0)),
        scratch_shapes=[pltpu.VMEM((2, D_EXPERT, D_MODEL), BF16),
                        pltpu.VMEM((2, rows, D_MODEL), F32), pltpu.SemaphoreType.DMA((2,))],
    )
    return pl.pallas_call(
        _moe_down_kernel,
        grid_spec=grid_spec,
        out_shape=jax.ShapeDtypeStruct((nblk * MOE_TM, D_MODEL), F32),
        compiler_params=_cparams(("arbitrary",)),
        name="moe_down",
    )(*sched, h, wd, bd)


def _combine_kernel(pos_ref, posn_ref, gate_ref, x1_ref, g2_ref, b2_ref, y_hbm, o_ref, ybuf, sem):
    i = pl.program_id(0)
    n = pl.num_programs(0)
    slot = i % 2
    nrow = TOP_K * COMB_TM

    @pl.when(i == 0)
    def _():
        _row_gather(pos_ref, nrow, y_hbm, ybuf.at[0], sem.at[0])

    @pl.when(i + 1 < n)
    def _():
        _row_gather_unrolled(posn_ref, nrow, y_hbm, ybuf.at[1 - slot], sem.at[1 - slot], both_queues=True)

    pltpu.make_async_copy(y_hbm.at[pl.ds(0, nrow)], ybuf.at[slot], sem.at[slot]).wait()
    gate = gate_ref[...]
    y = jnp.zeros((COMB_TM, D_MODEL), F32)
    for kk in range(TOP_K):
        y = y + gate[:, kk:kk + 1] * ybuf[slot, kk * COMB_TM:(kk + 1) * COMB_TM, :]
    o_ref[...] = _layer_norm(DEEPNORM_ALPHA * x1_ref[...] + y, g2_ref[...], b2_ref[...])


def _combine(pos3, gate, x1, g2, b2, yb):
    S = x1.shape[0]
    nt = S // COMB_TM
    nrow = TOP_K * COMB_TM
    smem_blk = lambda off: pl.BlockSpec(
        (1, 1, nrow), lambda i: (jnp.minimum(i + off, nt - 1), 0, 0), memory_space=pltpu.SMEM)
    return pl.pallas_call(
        _combine_kernel,
        grid=(nt,),
        in_specs=[smem_blk(0), smem_blk(1),
                  pl.BlockSpec((COMB_TM, LANES), lambda i: (i, 0)),
                  pl.BlockSpec((COMB_TM, D_MODEL), lambda i: (i, 0)),
                  pl.BlockSpec((1, D_MODEL), lambda i: (0, 0)),
                  pl.BlockSpec((1, D_MODEL), lambda i: (0, 0)),
                  pl.BlockSpec(memory_space=pl.ANY)],
        out_specs=pl.BlockSpec((COMB_TM, D_MODEL), lambda i: (i, 0)),
        out_shape=jax.ShapeDtypeStruct((S, D_MODEL), F32),
        scratch_shapes=[pltpu.VMEM((2, nrow, D_MODEL), F32), pltpu.SemaphoreType.DMA((2,))],
        compiler_params=_cparams(("arbitrary",)),
        name="combine",
    )(pos3, pos3, gate, x1, g2, b2, yb)


def _rope_tables(pos):
    half = ROT_DIM // 2
    inv_freq = ROPE_THETA ** (-jnp.arange(0, ROT_DIM, 2, dtype=F32) / ROT_DIM)
    ang = pos.astype(F32)[:, None] * inv_freq
    cos, sin = jnp.cos(ang), jnp.sin(ang)
    S = pos.shape[0]
    one = jnp.ones((S, HEAD_DIM - ROT_DIM), F32)
    zero = jnp.zeros((S, HEAD_DIM - ROT_DIM), F32)
    z16 = jnp.zeros((S, half), F32)
    rc = jnp.concatenate([cos, cos, one], axis=1)
    ra = jnp.concatenate([-sin, z16, zero], axis=1)
    rb = jnp.concatenate([z16, sin, zero], axis=1)
    return rc, ra, rb


def _invert_kernel(dest_ref, pad_lo_ref, pad_hi_ref, rt_ref):
    i = pl.program_id(0)

    @pl.when(i == 0)
    def _():
        for e in range(pad_lo_ref.shape[0]):
            def fill(p, carry):
                rt_ref[p] = 0
                return carry
            lax.fori_loop(pad_lo_ref[e], pad_hi_ref[e], fill, 0)

    base = i * INV_BLK

    def body(jb, carry):
        j0 = jb * IDX_BATCH
        ds = [dest_ref[0, 0, j0 + u] for u in range(IDX_BATCH)]
        for u, d in enumerate(ds):
            rt_ref[d] = lax.shift_right_logical(base + j0 + u, TOP_K.bit_length() - 1)
        return carry
    lax.fori_loop(0, INV_BLK // IDX_BATCH, body, 0)


def _invert_rows(dest, pad_lo, pad_hi, nrows):
    n = dest.size
    return pl.pallas_call(
        _invert_kernel,
        grid=(n // INV_BLK,),
        in_specs=[pl.BlockSpec((1, 1, INV_BLK), lambda i: (i, 0, 0), memory_space=pltpu.SMEM),
                  pl.BlockSpec(memory_space=pltpu.SMEM),
                  pl.BlockSpec(memory_space=pltpu.SMEM)],
        out_specs=pl.BlockSpec(memory_space=pltpu.SMEM),
        out_shape=jax.ShapeDtypeStruct((nrows,), jnp.int32),
        compiler_params=_cparams(("arbitrary",)),
        name="invert_rows",
    )(dest.reshape(n // INV_BLK, 1, INV_BLK), pad_lo, pad_hi)


def _routing(idx, rank, counts, S):
    n = S * TOP_K
    nblk = n // MOE_TM + N_EXPERTS
    onehot = (idx[:, :, None] == jnp.arange(N_EXPERTS, dtype=jnp.int32)[None, None, :])
    padded = (counts + MOE_TM - 1) // MOE_TM * MOE_TM
    pends = jnp.cumsum(padded)
    pstarts = pends - padded
    dest = jnp.sum(jnp.where(onehot, pstarts[None, None, :], 0), axis=-1) + rank
    nrows = nblk * MOE_TM
    pad_lo = jnp.concatenate([pstarts + counts, pends[-1:]]).astype(jnp.int32)
    pad_hi = jnp.concatenate([pends, jnp.full((1,), nrows, pends.dtype)]).astype(jnp.int32)
    row_tok = _invert_rows(dest.astype(jnp.int32), pad_lo, pad_hi, nrows)
    starts = jnp.arange(nblk, dtype=jnp.int32) * MOE_TM
    block_e = jnp.minimum(jnp.sum(pends[None, :] <= starts[:, None], axis=1, dtype=jnp.int32), N_EXPERTS - 1)
    nused = (pends[-1:] // MOE_TM).astype(jnp.int32)

    e_ids = jnp.arange(N_EXPERTS, dtype=jnp.int32)
    nb_e = (padded // MOE_TM).astype(jnp.int32)
    bstart_e = (pstarts // MOE_TM).astype(jnp.int32)
    nonempty = nb_e > 0
    run_of_e = jnp.cumsum(nonempty.astype(jnp.int32)) - 1
    nruns = jnp.sum(nonempty.astype(jnp.int32))
    run_e = jnp.sum(jnp.where(nonempty[None, :] & (run_of_e[None, :] == e_ids[:, None]), e_ids[None, :], 0), axis=1)
    task_e = jnp.repeat(run_e, W_CHUNKS).astype(jnp.int32)
    ntask = (W_CHUNKS * nruns).reshape(1).astype(jnp.int32)
    blk = jnp.arange(nblk, dtype=jnp.int32)
    used = blk < nused[0]
    of_block = block_e[:, None] == e_ids[None, :]
    pick = lambda v: jnp.sum(jnp.where(of_block, v[None, :], 0), axis=1)
    r_i = pick(run_of_e)
    j_i = blk - pick(bstart_e)
    n_i = jnp.maximum(pick(nb_e), 1)
    steps = jnp.arange(1, W_CHUNKS + 1, dtype=jnp.int32)[None, :]
    share = lambda j: jnp.sum(steps * n_i[:, None] <= W_CHUNKS * j[:, None], axis=1, dtype=jnp.int32)
    t0 = W_CHUNKS * (r_i + 1) + share(j_i)
    t1 = W_CHUNKS * (r_i + 1) + share(j_i + 1)
    t0 = jnp.where(blk == 0, 0, jnp.where(used, jnp.minimum(t0, ntask[0]), ntask[0])).astype(jnp.int32)
    t1 = jnp.where(used, jnp.minimum(t1, ntask[0]), ntask[0]).astype(jnp.int32)
    par = (r_i % 2).astype(jnp.int32)
    sched = (block_e, nused, par, t0, t1, task_e, ntask)
    return dest.astype(jnp.int32), row_tok, sched


def kernel(x, positions, w_in, w_attn_out, sgu_ln_g, sgu_ln_b, sgu_w, sgu_b, w_sgu_out, w_out,
           ln1_g, ln1_b, w_router, b_router, w_gate_up, b_gate_up, w_down, b_down, ln2_g, ln2_b):
    B, S, D = x.shape
    assert B == 1 and D == D_MODEL and w_in.shape[0] == DEPTH
    assert S % PROJ_TM == 0 and S % (DIL_PATTERNS[-1][1] * ATTN_QB * ATTN_BLK) == 0
    x2d = x.reshape(S, D)

    a0, a1, a2 = ATTN_WIDTH, 2 * ATTN_WIDTH, 3 * ATTN_WIDTH
    z1 = a2 + 2 * SGU_WIDTH
    wi = w_in[0]
    qkv_cols = [wi[:, p * ATTN_WIDTH + gi * GROUP_W:p * ATTN_WIDTH + (gi + 1) * GROUP_W]
                for gi in range(N_GROUPS) for p in range(3)]
    w_cat = jnp.concatenate([wi[:, a2:z1], wi[:, z1:z1 + D], wi[:, z1 + D:]] + qkv_cols, axis=1).astype(BF16)
    rc, ra, rb = _rope_tables(positions[0])

    main, qkv2, qkv3 = _inproj(x2d, w_cat, rc, ra, rb)
    for window, d in DIL_PATTERNS:
        assert window // (2 * d) == ATTN_BLK // 2
    o1, l1 = _attention(main.reshape(1, S, main.shape[1]), QKV_BLK0, QKV_BLK0 + 1, QKV_BLK0 + 2)
    o2, l2 = _attention(qkv2, 0, 1, 2)
    o3, l3 = _attention(qkv3, 0, 1, 2)

    sbias = jnp.repeat(sgu_b[0].T, CHUNK, axis=1)
    wr = jnp.pad(w_router[0], ((0, 0), (0, LANES - N_EXPERTS))).astype(BF16)
    br = jnp.pad(b_router[0], (0, LANES - N_EXPERTS)).reshape(1, LANES)
    x1, idx_f, gate, rank_f, cnt_f = _tail(
        main, o1.reshape(S, GROUP_W), o2, o3, l1.reshape(S, LANES), l2, l3, x2d,
        sgu_w[0].astype(BF16), sbias, sgu_ln_g[0].reshape(1, -1), sgu_ln_b[0].reshape(1, -1),
        w_sgu_out[0].astype(BF16), w_attn_out[0].astype(BF16), w_out[0].astype(BF16),
        ln1_g[0].reshape(1, -1), ln1_b[0].reshape(1, -1), wr, br)

    idx = idx_f[:, :TOP_K].astype(jnp.int32)
    rank = rank_f[:, :TOP_K].astype(jnp.int32)
    counts = cnt_f[0, :N_EXPERTS].astype(jnp.int32)
    dest, row_tok, sched = _routing(idx, rank, counts, S)
    nblk = sched[0].shape[0]

    h = _moe_up(sched, row_tok.reshape(nblk, 1, MOE_TM), x1, w_gate_up[0], b_gate_up[0].reshape(N_EXPERTS, 1, -1))
    yb = _moe_down(sched, h, w_down[0], b_down[0].reshape(N_EXPERTS, 1, -1))

    nt = S // COMB_TM
    pos3 = dest.reshape(nt, COMB_TM, TOP_K).transpose(0, 2, 1).reshape(nt, 1, TOP_K * COMB_TM)
    out = _combine(pos3, gate, x1, ln2_g[0].reshape(1, -1), ln2_b[0].reshape(1, -1), yb)
    return out.reshape(B, S, D)
```
